```python
import jax, jax.numpy as jnp
from jax import lax
import numpy as np

D_MODEL = 1024
BATCH = 8
SEQ = 4096
DEPTH = 4

ROPE_THETA = 10000.0
NORM_EPS = 1e-6
NEG_INF = -1e30

A_HEADS = 8
A_GROUPS = 2
A_HPG = A_HEADS // A_GROUPS
A_HEAD_DIM = 64
CMP_LEN = 32
CMP_STRIDE = 16
SEL_LEN = 64
SEL_TOPK = 16
WINDOW = 512
Q_CHUNK = 64

B_HEADS = 8
B_Q_LORA = 384
B_KV_LORA = 256
B_NOPE = 64
B_ROPE = 32
B_V = 64
B_QK = B_NOPE + B_ROPE
ATTN_BLOCK = 128

C_WIDTH = 1024
C_BLOCKS = 8
C_BLOCK_W = C_WIDTH // C_BLOCKS
C_CONV = 4
C_SCALE = 8.0

FFN_HIDDEN = -(-8 * D_MODEL // (3 * 256)) * 256

A_Q = A_HEADS * A_HEAD_DIM
A_KV = A_GROUPS * A_HEAD_DIM
A_GATES = 3 * A_HEADS
IN_SPLITS = (A_Q, A_KV, A_KV, A_KV, A_KV, A_KV, A_KV, A_GATES,
             B_Q_LORA, B_KV_LORA, B_ROPE,
             C_WIDTH, C_WIDTH,
             D_MODEL, D_MODEL, D_MODEL)
N_IN = sum(IN_SPLITS)
A_OUT = A_HEADS * A_HEAD_DIM
B_OUT = B_HEADS * B_V

kernel_name = 'hybrid_nsa_mla_rglru_gated_block'


def rms_norm(x, g):
    xf = x.astype(jnp.float32)
    y = xf * lax.rsqrt(jnp.mean(xf * xf, axis=-1, keepdims=True) + NORM_EPS)
    return (y * g.astype(jnp.float32)).astype(x.dtype)


def rope(x, pos):
    half = x.shape[-1] // 2
    inv = ROPE_THETA ** (-jnp.arange(half, dtype=jnp.float32) / half)
    ang = pos.astype(jnp.float32)[..., None] * inv
    cos = jnp.cos(ang)[:, :, None, :]
    sin = jnp.sin(ang)[:, :, None, :]
    xf = x.astype(jnp.float32)
    x1, x2 = xf[..., :half], xf[..., half:]
    return jnp.concatenate([x1 * cos - x2 * sin, x2 * cos + x1 * sin], axis=-1).astype(x.dtype)


def masked_softmax(s, mask):
    s = jnp.where(mask, s.astype(jnp.float32), NEG_INF)
    p = jax.nn.softmax(s, axis=-1)
    return jnp.where(mask, p, 0.0)


def nsa_mixer(q, k_cmp, v_cmp, k_sel, v_sel, k_win, v_win, gate_logits, pos,
              q_norm_g, k_norm_g, cmp_pos, cmp_w1, cmp_w2):
    bsz, S, _ = q.shape
    dt = q.dtype
    G, HPG, dh = A_GROUPS, A_HPG, A_HEAD_DIM
    scale = dh ** -0.5
    t = jnp.arange(S, dtype=jnp.int32)
    q = rms_norm(q.reshape(bsz, S, A_HEADS, dh), q_norm_g)

    n_cmp = (S - CMP_LEN) // CMP_STRIDE + 1
    starts = np.arange(n_cmp) * CMP_STRIDE
    blk_idx = starts[:, None] + np.arange(CMP_LEN)[None, :]
    kv_raw = jnp.stack([k_cmp, v_cmp], 0).reshape(2, bsz, S, G, dh)
    blocks = jnp.take(kv_raw, blk_idx, axis=2)
    blocks = blocks + cmp_pos[:, None, None, :, None, :]
    blocks = blocks.transpose(0, 1, 2, 4, 3, 5).reshape(2, bsz, n_cmp, G, CMP_LEN * dh)
    hid = jax.nn.silu(jnp.einsum('zbngf,zfe->zbnge', blocks, cmp_w1))
    comp = jnp.einsum('zbnge,zef->zbngf', hid, cmp_w2)
    k_c = rms_norm(comp[0], k_norm_g)
    v_c = comp[1]
    qg = q.reshape(bsz, S, G, HPG, dh)
    s_c = jnp.einsum('bsghd,bngd->bghsn', qg, k_c) * scale
    mask_c = t[:, None] >= jnp.asarray(starts + CMP_LEN - 1, dtype=jnp.int32)[None, :]
    p_c = masked_softmax(s_c, mask_c)
    o_c = jnp.einsum('bghsn,bngd->bsghd', p_c.astype(dt), v_c)

    n_sel = S // SEL_LEN
    k_top = min(SEL_TOPK, n_sel)
    sel_starts = np.arange(n_sel) * SEL_LEN
    ovl = np.clip(np.minimum(starts[:, None] + CMP_LEN, sel_starts[None, :] + SEL_LEN)
                  - np.maximum(starts[:, None], sel_starts[None, :]), 0, None)
    ovl = jnp.asarray(ovl / CMP_LEN, dtype=jnp.float32)
    imp = jnp.einsum('bghsn,nj->bgsj', p_c, ovl)
    cur = (t // SEL_LEN)[:, None]
    j = jnp.arange(n_sel, dtype=jnp.int32)[None, :]
    forced = (j == 0) | (j == cur) | (j == cur - 1)
    imp = jnp.where(j > cur, NEG_INF, jnp.where(forced, -NEG_INF, imp))
    _, sel_idx = lax.top_k(imp, k_top)

    q_r = rope(q, pos).reshape(bsz, S, G, HPG, dh)
    k_s = rope(rms_norm(k_sel.reshape(bsz, S, G, dh), k_norm_g), pos)
    k_w = rope(rms_norm(k_win.reshape(bsz, S, G, dh), k_norm_g), pos)
    v_s = v_sel.reshape(bsz, S, G, dh)
    v_w = v_win.reshape(bsz, S, G, dh)
    k_s_blk = k_s.reshape(bsz, n_sel, SEL_LEN, G, dh).transpose(0, 3, 1, 2, 4)
    v_s_blk = v_s.reshape(bsz, n_sel, SEL_LEN, G, dh).transpose(0, 3, 1, 2, 4)
    pad = ((0, 0), (WINDOW, 0), (0, 0), (0, 0))
    k_w_pad = jnp.pad(k_w, pad)
    v_w_pad = jnp.pad(v_w, pad)
    b_ix = jnp.arange(bsz)[:, None, None, None]
    g_ix = jnp.arange(G)[None, :, None, None]
    sel_off = jnp.arange(SEL_LEN, dtype=jnp.int32)
    win_off = jnp.arange(WINDOW + Q_CHUNK, dtype=jnp.int32)
    n_keys = k_top * SEL_LEN

    def chunk(args):
        qc, idx, ci = args
        start = ci * Q_CHUNK
        tq = start + jnp.arange(Q_CHUNK, dtype=jnp.int32)
        ks = k_s_blk[b_ix, g_ix, idx].reshape(bsz, G, Q_CHUNK, n_keys, dh)
        vs = v_s_blk[b_ix, g_ix, idx].reshape(bsz, G, Q_CHUNK, n_keys, dh)
        kpos = (idx[..., None] * SEL_LEN + sel_off).reshape(bsz, G, Q_CHUNK, n_keys)
        s = jnp.einsum('bqghd,bgqkd->bghqk', qc, ks) * scale
        p = masked_softmax(s, kpos[:, :, None] <= tq[:, None])
        o_s = jnp.einsum('bghqk,bgqkd->bqghd', p.astype(dt), vs)
        kw = lax.dynamic_slice_in_dim(k_w_pad, start, WINDOW + Q_CHUNK, axis=1)
        vw = lax.dynamic_slice_in_dim(v_w_pad, start, WINDOW + Q_CHUNK, axis=1)
        wpos = start - WINDOW + win_off
        mask_w = (wpos[None, :] <= tq[:, None]) & (wpos[None, :] > tq[:, None] - WINDOW) & (wpos[None, :] >= 0)
        s_w = jnp.einsum('bqghd,bkgd->bghqk', qc, kw) * scale
        p_w = masked_softmax(s_w, mask_w)
        o_w = jnp.einsum('bghqk,bkgd->bqghd', p_w.astype(dt), vw)
        return o_s, o_w

    n_chunks = S // Q_CHUNK
    xs = (q_r.reshape(bsz, n_chunks, Q_CHUNK, G, HPG, dh).swapaxes(0, 1),
          sel_idx.reshape(bsz, G, n_chunks, Q_CHUNK, k_top).transpose(2, 0, 1, 3, 4),
          jnp.arange(n_chunks, dtype=jnp.int32))
    o_s, o_w = lax.map(chunk, xs)
    o_s = o_s.swapaxes(0, 1).reshape(bsz, S, G, HPG, dh)
    o_w = o_w.swapaxes(0, 1).reshape(bsz, S, G, HPG, dh)

    g = jax.nn.sigmoid(gate_logits).reshape(bsz, S, G, HPG, 3)
    o = g[..., 0:1] * o_c + g[..., 1:2] * o_s + g[..., 2:3] * o_w
    return o.reshape(bsz, S, A_OUT)


def mla_mixer(c_q, c_kv, k_pe, pos, cq_norm_g, ckv_norm_g, w_uq, w_ukv, q_norm_g, k_norm_g):
    bsz, S, _ = c_q.shape
    dt = c_q.dtype
    q = (rms_norm(c_q, cq_norm_g) @ w_uq).reshape(bsz, S, B_HEADS, B_QK)
    kv = (rms_norm(c_kv, ckv_norm_g) @ w_ukv).reshape(bsz, S, B_HEADS, B_NOPE + B_V)
    k_nope, v = kv[..., :B_NOPE], kv[..., B_NOPE:]
    k = jnp.concatenate([k_nope, jnp.broadcast_to(k_pe[:, :, None, :], (bsz, S, B_HEADS, B_ROPE))], axis=-1)
    q = rms_norm(q, q_norm_g)
    k = rms_norm(k, k_norm_g)
    q = jnp.concatenate([q[..., :B_NOPE], rope(q[..., B_NOPE:], pos)], axis=-1)
    k = jnp.concatenate([k[..., :B_NOPE], rope(k[..., B_NOPE:], pos)], axis=-1)
    scale = B_QK ** -0.5
    kpos = jnp.arange(S, dtype=jnp.int32)
    n_blk = S // ATTN_BLOCK

    def blk(args):
        qi, bi = args
        tq = bi * ATTN_BLOCK + jnp.arange(ATTN_BLOCK, dtype=jnp.int32)
        s = jnp.einsum('bqhd,bkhd->bhqk', qi, k) * scale
        p = masked_softmax(s, kpos[None, :] <= tq[:, None])
        return jnp.einsum('bhqk,bkhd->bqhd', p.astype(dt), v)

    qb = q.reshape(bsz, n_blk, ATTN_BLOCK, B_HEADS, B_QK).swapaxes(0, 1)
    o = lax.map(blk, (qb, jnp.arange(n_blk, dtype=jnp.int32)))
    return o.swapaxes(0, 1).reshape(bsz, S, B_OUT)


def rglru_mixer(gate, xr, conv_w, conv_b, w_a, b_a, w_x, b_x, lam):
    bsz, S, C = xr.shape
    dt = xr.dtype
    xc = lax.conv_general_dilated(xr, conv_w[:, None, :], window_strides=(1,),
                                  padding=[(C_CONV - 1, 0)],
                                  dimension_numbers=('NWC', 'WIO', 'NWC'),
                                  feature_group_count=C) + conv_b
    xb = xc.reshape(bsz, S, C_BLOCKS, C_BLOCK_W)
    r = jax.nn.sigmoid(jnp.einsum('bsnc,ncd->bsnd', xb, w_a).reshape(bsz, S, C) + b_a)
    i = jax.nn.sigmoid(jnp.einsum('bsnc,ncd->bsnd', xb, w_x).reshape(bsz, S, C) + b_x)
    log_a = -C_SCALE * r.astype(jnp.float32) * jax.nn.softplus(-lam.astype(jnp.float32))
    a = jnp.exp(log_a)
    b = jnp.sqrt(-jnp.expm1(2.0 * log_a)) * (i.astype(jnp.float32) * xc.astype(jnp.float32))

    def combine(e1, e2):
        a1, b1 = e1
        a2, b2 = e2
        return a1 * a2, a2 * b1 + b2

    _, h = lax.associative_scan(combine, (a, b), axis=1)
    return jax.nn.gelu(gate) * h.astype(dt)


def setup_inputs(seed: int = 0) -> dict:
    key = jax.random.key(seed)
    ks = jax.random.split(key, 30)
    f32 = jnp.float32
    L = DEPTH

    def nrm(k, shape, fan_in):
        return jax.random.normal(k, shape, f32) * (fan_in ** -0.5)

    def gain(k, shape):
        return 1.0 + 0.01 * jax.random.normal(k, shape, f32)

    x = jax.random.normal(ks[0], (BATCH, SEQ, D_MODEL), f32)
    positions = (jnp.arange(SEQ, dtype=jnp.int32)[None, :]
                 + jax.random.randint(ks[1], (BATCH, 1), 0, 1024, dtype=jnp.int32))
    u = jax.random.uniform(ks[21], (L, C_WIDTH), f32, 0.9, 0.999)
    s_lam = u ** (1.0 / C_SCALE)
    return {
        'x': x,
        'positions': positions,
        'mix_norm_g': gain(ks[2], (L, D_MODEL)),
        'w_in': nrm(ks[3], (L, D_MODEL, N_IN), D_MODEL),
        'a_q_norm_g': gain(ks[4], (L, A_HEAD_DIM)),
        'a_k_norm_g': gain(ks[5], (L, A_HEAD_DIM)),
        'a_cmp_pos': 0.02 * jax.random.normal(ks[6], (L, 2, CMP_LEN, A_HEAD_DIM), f32),
        'a_cmp_w1': nrm(ks[7], (L, 2, CMP_LEN * A_HEAD_DIM, A_HEAD_DIM), CMP_LEN * A_HEAD_DIM),
        'a_cmp_w2': nrm(ks[8], (L, 2, A_HEAD_DIM, A_HEAD_DIM), A_HEAD_DIM),
        'b_cq_norm_g': gain(ks[9], (L, B_Q_LORA)),
        'b_ckv_norm_g': gain(ks[10], (L, B_KV_LORA)),
        'b_w_uq': nrm(ks[11], (L, B_Q_LORA, B_HEADS * B_QK), B_Q_LORA),
        'b_w_ukv': nrm(ks[12], (L, B_KV_LORA, B_HEADS * (B_NOPE + B_V)), B_KV_LORA),
        'b_q_norm_g': gain(ks[13], (L, B_QK)),
        'b_k_norm_g': gain(ks[14], (L, B_QK)),
        'c_conv_w': nrm(ks[15], (L, C_CONV, C_WIDTH), C_CONV),
        'c_conv_b': 0.01 * jax.random.normal(ks[16], (L, C_WIDTH), f32),
        'c_w_a': nrm(ks[17], (L, C_BLOCKS, C_BLOCK_W, C_BLOCK_W), C_BLOCK_W),
        'c_b_a': 0.01 * jax.random.normal(ks[18], (L, C_WIDTH), f32),
        'c_w_x': nrm(ks[19], (L, C_BLOCKS, C_BLOCK_W, C_BLOCK_W), C_BLOCK_W),
        'c_b_x': 0.01 * jax.random.normal(ks[20], (L, C_WIDTH), f32),
        'c_lambda': jnp.log(s_lam) - jnp.log1p(-s_lam),
        'w_pa': nrm(ks[22], (L, A_OUT, D_MODEL), A_OUT),
        'w_pb': nrm(ks[23], (L, B_OUT, D_MODEL), B_OUT),
        'w_pc': nrm(ks[24], (L, C_WIDTH, D_MODEL), C_WIDTH),
        'w_o': nrm(ks[25], (L, D_MODEL, D_MODEL), D_MODEL),
        'ffn_norm_g': gain(ks[26], (L, D_MODEL)),
        'ffn_w1': nrm(ks[27], (L, D_MODEL, FFN_HIDDEN), D_MODEL),
        'ffn_w3': nrm(ks[28], (L, D_MODEL, FFN_HIDDEN), D_MODEL),
        'ffn_w2': nrm(ks[29], (L, FFN_HIDDEN, D_MODEL), FFN_HIDDEN),
    }


def reference(x, positions, mix_norm_g, w_in, a_q_norm_g, a_k_norm_g, a_cmp_pos, a_cmp_w1, a_cmp_w2,
              b_cq_norm_g, b_ckv_norm_g, b_w_uq, b_w_ukv, b_q_norm_g, b_k_norm_g,
              c_conv_w, c_conv_b, c_w_a, c_b_a, c_w_x, c_b_x, c_lambda,
              w_pa, w_pb, w_pc, w_o, ffn_norm_g, ffn_w1, ffn_w3, ffn_w2):
    offsets = np.cumsum(IN_SPLITS)[:-1].tolist()
    for l in range(DEPTH):
        h = rms_norm(x, mix_norm_g[l])
        proj = h @ w_in[l]
        (aq, akc, avc, aks, avs, akw, avw, ag,
         bcq, bckv, bkpe, cg, cx, ma, mb, mc) = jnp.split(proj, offsets, axis=-1)
        ya = nsa_mixer(aq, akc, avc, aks, avs, akw, avw, ag, positions,
                       a_q_norm_g[l], a_k_norm_g[l], a_cmp_pos[l], a_cmp_w1[l], a_cmp_w2[l])
        yb = mla_mixer(bcq, bckv, bkpe, positions, b_cq_norm_g[l], b_ckv_norm_g[l],
                       b_w_uq[l], b_w_ukv[l], b_q_norm_g[l], b_k_norm_g[l])
        yc = rglru_mixer(cg, cx, c_conv_w[l], c_conv_b[l], c_w_a[l], c_b_a[l],
                         c_w_x[l], c_b_x[l], c_lambda[l])
        merged = (jax.nn.sigmoid(ma) * (ya @ w_pa[l])
                  + jax.nn.sigmoid(mb) * (yb @ w_pb[l])
                  + jax.nn.sigmoid(mc) * (yc @ w_pc[l]))
        x = x + merged @ w_o[l]
        h = rms_norm(x, ffn_norm_g[l])
        x = x + (jax.nn.silu(h @ ffn_w1[l]) * (h @ ffn_w3[l])) @ ffn_w2[l]
    return x
```

```python
import functools

import numpy as np
import jax
import jax.numpy as jnp
from jax import lax
from jax.experimental import pallas as pl
from jax.experimental.pallas import tpu as pltpu

F32 = jnp.float32
BF16 = jnp.bfloat16

D_MODEL = 1024
ROPE_THETA = 10000.0
NORM_EPS = 1e-6
NEG_INF = -1e30

A_HEADS = 8
A_GROUPS = 2
A_HPG = A_HEADS // A_GROUPS
A_HEAD_DIM = 64
CMP_LEN = 32
CMP_STRIDE = 16
SEL_LEN = 64
SEL_TOPK = 16
WINDOW = 512

B_HEADS = 8
B_Q_LORA = 384
B_KV_LORA = 256
B_NOPE = 64
B_ROPE = 32
B_V = 64
B_QK = B_NOPE + B_ROPE

C_WIDTH = 1024
C_BLOCKS = 8
C_BLOCK_W = C_WIDTH // C_BLOCKS
C_CONV = 4
C_SCALE = 8.0

FFN_HIDDEN = 2816

A_Q = A_HEADS * A_HEAD_DIM
A_KV = A_GROUPS * A_HEAD_DIM
A_GATES = 3 * A_HEADS

LANE = 128
SLOT = LANE
VMEM_LIMIT = 56 * 1024 * 1024

PA_Q = 0
PA_KC = A_HEADS * SLOT
PA_VC = PA_KC + A_KV
PA_KS = PA_VC + A_KV
PA_VS = PA_KS + A_GROUPS * SLOT
PA_KW = PA_VS + A_GROUPS * SLOT
PA_VW = PA_KW + A_GROUPS * SLOT
PA_G = PA_VW + A_GROUPS * SLOT
PA_N = PA_G + 2 * SLOT

PB_KV = 0
PB_PE = B_KV_LORA
PB_Q = PB_PE + SLOT
PB_N = PB_Q + B_Q_LORA


def _cparams(sem):
    return pltpu.CompilerParams(dimension_semantics=sem, vmem_limit_bytes=VMEM_LIMIT)


def _dot(a, b):
    return jnp.dot(a, b, preferred_element_type=F32)


def _dot_nt(a, b):
    return lax.dot_general(a, b, (((1,), (1,)), ((), ())), preferred_element_type=F32)


def _norm_matmul_kernel(x_ref, g_ref, w_ref, o_ref, h_ref):
    @pl.when(pl.program_id(1) == 0)
    def _():
        x = x_ref[...]
        inv = lax.rsqrt(jnp.mean(x * x, axis=-1, keepdims=True) + NORM_EPS)
        h_ref[...] = ((x * inv) * g_ref[...]).astype(BF16)

    o_ref[...] = _dot(h_ref[...], w_ref[...])


def _pick_tn(n):
    if n <= 1024:
        return n
    for tn in (1280, 1024, 768, 640, 512, 384, 256, 128):
        if n % tn == 0:
            return tn
    raise ValueError(n)


def norm_matmul(x2d, g, w, tm=1024):
    t, k = x2d.shape
    n = w.shape[1]
    tn = _pick_tn(n)
    return pl.pallas_call(
        _norm_matmul_kernel,
        grid=(t // tm, n // tn),
        in_specs=[pl.BlockSpec((tm, k), lambda i, j: (i, 0)),
                  pl.BlockSpec((1, k), lambda i, j: (0, 0)),
                  pl.BlockSpec((k, tn), lambda i, j: (0, j))],
        out_specs=pl.BlockSpec((tm, tn), lambda i, j: (i, j)),
        out_shape=jax.ShapeDtypeStruct((t, n), F32),
        scratch_shapes=[pltpu.VMEM((tm, k), BF16)],
        compiler_params=_cparams(("parallel", "arbitrary")),
        name="norm_matmul",
    )(x2d, g.reshape(1, k), w)


def _slot_rms(x, g, width):
    ms = jnp.sum(x * x, axis=-1, keepdims=True) * (1.0 / width)
    return (x * lax.rsqrt(ms + NORM_EPS)) * g


def _slot_rope(y, cos, sin_lo, sin_hi, half):
    return y * cos + pltpu.roll(y, LANE - half, 1) * sin_lo + pltpu.roll(y, half, 1) * sin_hi


def _nsa_prep_kernel(aq_ref, ks_ref, vs_ref, kw_ref, vw_ref, cos_ref, slo_ref, shi_ref, qg_ref, kg_ref,
                     qn_o, qr_o, ks_o, vs_o, kw_o, vw_o):
    cos, slo, shi = cos_ref[...], slo_ref[...], shi_ref[...]
    qg, kg = qg_ref[...], kg_ref[...]
    scale = A_HEAD_DIM ** -0.5
    half = A_HEAD_DIM // 2
    for h in range(A_HEADS):
        sl = slice(h * SLOT, (h + 1) * SLOT)
        y = _slot_rms(aq_ref[:, sl], qg, A_HEAD_DIM)
        qn_o[:, sl] = (y * scale).astype(BF16)
        qr_o[:, sl] = (_slot_rope(y, cos, slo, shi, half) * scale).astype(BF16)
    for g in range(A_GROUPS):
        sl = slice(g * SLOT, (g + 1) * SLOT)
        ks_o[:, sl] = _slot_rope(_slot_rms(ks_ref[:, sl], kg, A_HEAD_DIM), cos, slo, shi, half).astype(BF16)
        kw_o[:, sl] = _slot_rope(_slot_rms(kw_ref[:, sl], kg, A_HEAD_DIM), cos, slo, shi, half).astype(BF16)
    vs_o[...] = vs_ref[...].astype(BF16)
    vw_o[...] = vw_ref[...].astype(BF16)


def nsa_prep(pa, tabs, qg, kg, tm=512):
    bsz, s, _ = pa.shape
    gw = A_GROUPS * SLOT

    def col(width, off):
        return pl.BlockSpec((None, tm, width), lambda b, i: (b, i, off // width))

    tab = pl.BlockSpec((None, tm, LANE), lambda b, i: (b, i, 0))
    vec = pl.BlockSpec((1, LANE), lambda b, i: (0, 0))
    out_q = pl.BlockSpec((None, tm, A_HEADS * SLOT), lambda b, i: (b, i, 0))
    out_kv = pl.BlockSpec((None, tm, gw), lambda b, i: (b, i, 0))
    sq = jax.ShapeDtypeStruct((bsz, s, A_HEADS * SLOT), BF16)
    skv = jax.ShapeDtypeStruct((bsz, s, gw), BF16)
    return pl.pallas_call(
        _nsa_prep_kernel,
        grid=(bsz, s // tm),
        in_specs=[col(A_HEADS * SLOT, PA_Q), col(gw, PA_KS), col(gw, PA_VS), col(gw, PA_KW), col(gw, PA_VW),
                  tab, tab, tab, vec, vec],
        out_specs=[out_q, out_q, out_kv, out_kv, out_kv, out_kv],
        out_shape=[sq, sq, skv, skv, skv, skv],
        compiler_params=_cparams(("parallel", "parallel")),
        name="nsa_prep",
    )(pa, pa, pa, pa, pa, *tabs, qg, kg)


def _compress_kernel(uk_ref, uv_ref, w1_ref, w2_ref, pos_ref, kg_ref, kc_o, vc_o):
    half = CMP_STRIDE * A_HEAD_DIM
    n = uk_ref.shape[0]
    for z, (u_ref, o_ref) in enumerate(((uk_ref, kc_o), (uv_ref, vc_o))):
        u = u_ref[...].astype(BF16)
        first = _dot(u, w1_ref[z, :half, :])
        second = _dot(u, w1_ref[z, half:, :])
        posc = _dot(pos_ref[z], w1_ref[z])[0:1, :]
        hid = first + pltpu.roll(second, n - 1, 0) + posc
        hid = hid * jax.nn.sigmoid(hid)
        comp = _dot(hid.astype(BF16), w2_ref[z])
        if z == 0:
            comp = _slot_rms(comp, kg_ref[...], A_HEAD_DIM)
        o_ref[...] = comp.astype(BF16)


def nsa_compress(uk, uv, w1, w2, pos, kg):
    bsz, g, n, f = uk.shape
    u_spec = pl.BlockSpec((None, None, n, f), lambda b, gi: (b, gi, 0, 0))
    o_spec = pl.BlockSpec((None, None, n, SLOT), lambda b, gi: (b, gi, 0, 0))
    so = jax.ShapeDtypeStruct((bsz, g, n, SLOT), BF16)
    return pl.pallas_call(
        _compress_kernel,
        grid=(bsz, g),
        in_specs=[u_spec, u_spec,
                  pl.BlockSpec(w1.shape, lambda b, gi: (0, 0, 0)),
                  pl.BlockSpec(w2.shape, lambda b, gi: (0, 0, 0)),
                  pl.BlockSpec(pos.shape, lambda b, gi: (0, 0, 0)),
                  pl.BlockSpec((1, LANE), lambda b, gi: (0, 0))],
        out_specs=[o_spec, o_spec],
        out_shape=[so, so],
        compiler_params=_cparams(("parallel", "parallel")),
        name="nsa_compress",
    )(uk, uv, w1, w2, pos, kg)


def _cmp_attn_kernel(q_ref, kc_ref, vc_ref, ovl_ref, oc_o, sel_o, *, tq):
    t0 = pl.program_id(2) * tq
    kc = kc_ref[...]
    vc = vc_ref[...]
    n_cmp = kc.shape[0]
    n_sel = ovl_ref.shape[0]
    t = t0 + lax.broadcasted_iota(jnp.int32, (tq, n_cmp), 0)
    n = lax.broadcasted_iota(jnp.int32, (tq, n_cmp), 1)
    mask = t >= n * CMP_STRIDE + (CMP_LEN - 1)
    psum = jnp.zeros((tq, n_cmp), F32)
    for h in range(A_HPG):
        sl = slice(h * SLOT, (h + 1) * SLOT)
        s = jnp.where(mask, _dot_nt(q_ref[:, sl], kc), NEG_INF)
        m = jnp.max(s, axis=-1, keepdims=True)
        p = jnp.where(mask, jnp.exp(s - m), 0.0)
        l = jnp.sum(p, axis=-1, keepdims=True)
        p = p / jnp.where(l > 0.0, l, 1.0)
        psum = psum + p
        oc_o[:, sl] = _dot(p.astype(BF16), vc).astype(oc_o.dtype)
    imp = lax.dot_general(ovl_ref[...], psum, (((1,), (1,)), ((), ())),
                          precision=lax.Precision.HIGHEST, preferred_element_type=F32)
    j = lax.broadcasted_iota(jnp.int32, (n_sel, tq), 0)
    cur = (t0 + lax.broadcasted_iota(jnp.int32, (n_sel, tq), 1)) // SEL_LEN
    big = -NEG_INF
    imp = jnp.where(j == 0, 3.0 * big, jnp.where(j == cur, 2.0 * big, jnp.where(j == cur - 1, big, imp)))
    imp = jnp.where(j > cur, NEG_INF, imp)
    work = imp
    kth = None
    for _ in range(min(SEL_TOPK, n_sel)):
        kth = jnp.max(work, axis=0, keepdims=True)
        work = jnp.where(work >= kth, -jnp.inf, work)
    sel = jnp.where(imp >= kth, 1.0, 0.0)
    sel_o[...] = sel.T.astype(sel_o.dtype)


def nsa_cmp_attn(qn, kc, vc, ovl_t, tq=256):
    bsz, s, _ = qn.shape
    n_cmp = kc.shape[2]
    n_sel = ovl_t.shape[0]
    gq = A_HPG * SLOT
    kv_spec = pl.BlockSpec((None, None, n_cmp, SLOT), lambda b, g, i: (b, g, 0, 0))
    return pl.pallas_call(
        functools.partial(_cmp_attn_kernel, tq=tq),
        grid=(bsz, A_GROUPS, s // tq),
        in_specs=[pl.BlockSpec((None, tq, gq), lambda b, g, i: (b, i, g)),
                  kv_spec, kv_spec,
                  pl.BlockSpec(ovl_t.shape, lambda b, g, i: (0, 0))],
        out_specs=[pl.BlockSpec((None, tq, gq), lambda b, g, i: (b, i, g)),
                   pl.BlockSpec((None, None, tq, n_sel), lambda b, g, i: (b, g, i, 0))],
        out_shape=[jax.ShapeDtypeStruct((bsz, s, A_HEADS * SLOT), BF16),
                   jax.ShapeDtypeStruct((bsz, A_GROUPS, s, n_sel), BF16)],
        compiler_params=_cparams(("parallel", "parallel", "parallel")),
        name="nsa_cmp_attn",
    )(qn, kc, vc, ovl_t)


def _flash_kernel(*refs, hpg, mode, tq, tk, nk, back):
    if mode == "select":
        q_ref, k_ref, v_ref, sel_ref, o_ref, m_s, l_s, acc_s = refs
    else:
        q_ref, k_ref, v_ref, o_ref, m_s, l_s, acc_s = refs
    i = pl.program_id(2)
    j = pl.program_id(3)
    q0 = i * tq
    if mode == "window":
        kt = i * (tq // tk) - back + j
        active = kt >= 0
    else:
        kt = j
        active = kt * tk <= q0 + (tq - 1)
    k0 = kt * tk

    @pl.when(j == 0)
    def _():
        m_s[...] = jnp.full(m_s.shape, NEG_INF, F32)
        l_s[...] = jnp.zeros(l_s.shape, F32)
        acc_s[...] = jnp.zeros(acc_s.shape, F32)

    @pl.when(active)
    def _():
        k = k_ref[...]
        v = v_ref[...]
        rows = q0 + lax.broadcasted_iota(jnp.int32, (tq, tk), 0)
        cols = k0 + lax.broadcasted_iota(jnp.int32, (tq, tk), 1)
        mask = cols <= rows
        if mode == "window":
            mask = jnp.logical_and(mask, cols > rows - WINDOW)
        if mode == "select":
            n_sel = sel_ref.shape[-1]
            blk = lax.broadcasted_iota(jnp.int32, (n_sel, tk), 0)
            kblk = (k0 + lax.broadcasted_iota(jnp.int32, (n_sel, tk), 1)) // SEL_LEN
            expand = jnp.where(blk == kblk, 1.0, 0.0).astype(BF16)
            chosen = _dot(sel_ref[...], expand)
            mask = jnp.logical_and(mask, chosen > 0.5)
        for h in range(hpg):
            sl = slice(h * SLOT, (h + 1) * SLOT)
            s = jnp.where(mask, _dot_nt(q_ref[:, sl], k), NEG_INF)
            m_prev = m_s[h]
            m_new = jnp.maximum(m_prev, jnp.max(s, axis=-1, keepdims=True))
            alpha = jnp.exp(m_prev - m_new)
            p = jnp.where(mask, jnp.exp(s - m_new), 0.0)
            l_s[h] = alpha * l_s[h] + jnp.sum(p, axis=-1, keepdims=True)
            acc_s[h] = alpha * acc_s[h] + _dot(p.astype(BF16), v)
            m_s[h] = m_new

    @pl.when(j == nk - 1)
    def _():
        for h in range(hpg):
            o_ref[:, h * SLOT:(h + 1) * SLOT] = (acc_s[h] / l_s[h]).astype(o_ref.dtype)


def flash_attention(q, k, v, sel=None, *, hpg, mode, tq, tk):
    bsz, s, qw = q.shape
    groups = qw // (hpg * SLOT)
    if mode == "window":
        back = WINDOW // tk
        nk = back + tq // tk

        def k_idx(b, g, i, j):
            return (b, jnp.maximum(i * (tq // tk) - back + j, 0), g)
    else:
        back = 0
        nk = s // tk

        def k_idx(b, g, i, j):
            return (b, jnp.minimum(j, (i * tq + tq - 1) // tk), g)

    in_specs = [pl.BlockSpec((None, tq, hpg * SLOT), lambda b, g, i, j: (b, i, g)),
                pl.BlockSpec((None, tk, SLOT), k_idx),
                pl.BlockSpec((None, tk, SLOT), k_idx)]
    args = [q, k, v]
    if mode == "select":
        in_specs.append(pl.BlockSpec((None, None, tq, sel.shape[-1]), lambda b, g, i, j: (b, g, i, 0)))
        args.append(sel)
    return pl.pallas_call(
        functools.partial(_flash_kernel, hpg=hpg, mode=mode, tq=tq, tk=tk, nk=nk, back=back),
        grid=(bsz, groups, s // tq, nk),
        in_specs=in_specs,
        out_specs=pl.BlockSpec((None, tq, hpg * SLOT), lambda b, g, i, j: (b, i, g)),
        out_shape=jax.ShapeDtypeStruct((bsz, s, qw), BF16),
        scratch_shapes=[pltpu.VMEM((hpg, tq, 1), F32), pltpu.VMEM((hpg, tq, 1), F32),
                        pltpu.VMEM((hpg, tq, SLOT), F32)],
        compiler_params=_cparams(("parallel", "parallel", "parallel", "arbitrary")),
        name="flash_" + mode,
    )(*args)


def _mla_prep_kernel(ckv_ref, kpe_ref, cq_ref, cqg_ref, ckvg_ref, wuq_ref, wuk_ref, wuv_ref,
                     cos_ref, slo_ref, shi_ref, qg_ref, kg_ref, q_o, k_o, v_o):
    def rms(x, g):
        return ((x * lax.rsqrt(jnp.mean(x * x, axis=-1, keepdims=True) + NORM_EPS)) * g).astype(BF16)

    cq = rms(cq_ref[...], cqg_ref[...])
    ckv = rms(ckv_ref[...], ckvg_ref[...])
    q = _dot(cq, wuq_ref[...])
    kn = _dot(ckv, wuk_ref[...])
    v_o[...] = _dot(ckv, wuv_ref[...]).astype(BF16)
    kpe = kpe_ref[...]
    cos, slo, shi = cos_ref[...], slo_ref[...], shi_ref[...]
    qg, kg = qg_ref[...], kg_ref[...]
    scale = B_QK ** -0.5
    half = B_ROPE // 2
    for h in range(B_HEADS):
        sl = slice(h * SLOT, (h + 1) * SLOT)
        yq = _slot_rope(_slot_rms(q[:, sl], qg, B_QK), cos, slo, shi, half)
        q_o[:, sl] = (yq * scale).astype(BF16)
        yk = _slot_rope(_slot_rms(kn[:, sl] + kpe, kg, B_QK), cos, slo, shi, half)
        k_o[:, sl] = yk.astype(BF16)


def mla_prep(pb, cqg, ckvg, wuq, wuk, wuv, tabs, qg, kg, tm=256):
    bsz, s, _ = pb.shape
    hw = B_HEADS * SLOT

    def const(shape):
        return pl.BlockSpec(shape, lambda b, i: (0,) * len(shape))

    tab = pl.BlockSpec((None, tm, LANE), lambda b, i: (b, i, 0))
    out = pl.BlockSpec((None, tm, hw), lambda b, i: (b, i, 0))
    so = jax.ShapeDtypeStruct((bsz, s, hw), BF16)
    return pl.pallas_call(
        _mla_prep_kernel,
        grid=(bsz, s // tm),
        in_specs=[pl.BlockSpec((None, tm, B_KV_LORA), lambda b, i: (b, i, PB_KV // B_KV_LORA)),
                  pl.BlockSpec((None, tm, SLOT), lambda b, i: (b, i, PB_PE // SLOT)),
                  pl.BlockSpec((None, tm, B_Q_LORA), lambda b, i: (b, i, PB_Q // B_Q_LORA)),
                  const((1, B_Q_LORA)), const((1, B_KV_LORA)),
                  const(wuq.shape), const(wuk.shape), const(wuv.shape),
                  tab, tab, tab, const((1, LANE)), const((1, LANE))],
        out_specs=[out, out, out],
        out_shape=[so, so, so],
        compiler_params=_cparams(("parallel", "parallel")),
        name="mla_prep",
    )(pb, pb, pb, cqg, ckvg, wuq, wuk, wuv, *tabs, qg, kg)


SUBLANES = 8


def _rglru_kernel(cg_ref, cx_ref, cw_ref, cb_ref, wa_ref, ba_ref, wx_ref, bx_ref, lam_ref, o_ref,
                  xbuf, a_s, b_s, carry, *, ts):
    @pl.when(pl.program_id(1) == 0)
    def _():
        xbuf[0:SUBLANES, :] = jnp.zeros((SUBLANES, C_WIDTH), F32)
        carry[...] = jnp.zeros(carry.shape, F32)

    xr = cx_ref[...]
    xbuf[SUBLANES:SUBLANES + ts, :] = xr
    xc = cb_ref[...] + xr * cw_ref[C_CONV - 1:C_CONV, :]
    for w in range(C_CONV - 1):
        back = C_CONV - 1 - w
        xc = xc + xbuf[pl.ds(SUBLANES - back, ts), :] * cw_ref[w:w + 1, :]
    xbuf[0:SUBLANES, :] = xr[ts - SUBLANES:ts, :]

    for n in range(C_BLOCKS):
        sl = slice(n * C_BLOCK_W, (n + 1) * C_BLOCK_W)
        xb = xc[:, sl].astype(BF16)
        a_s[:, sl] = _dot(xb, wa_ref[n])
        b_s[:, sl] = _dot(xb, wx_ref[n])
    r = jax.nn.sigmoid(a_s[...] + ba_ref[...])
    ig = jax.nn.sigmoid(b_s[...] + bx_ref[...])
    nlam = -lam_ref[...]
    softplus = jnp.maximum(nlam, 0.0) + jnp.log1p(jnp.exp(-jnp.abs(nlam)))
    log_a = (-C_SCALE) * r * softplus
    a_s[...] = jnp.exp(log_a)
    th = jnp.tanh(log_a)
    b_s[...] = jnp.sqrt((-2.0 * th) / (1.0 - th)) * (ig * xc)

    row = lax.broadcasted_iota(jnp.int32, (SUBLANES, C_WIDTH), 0)

    def group(gi, h_prev):
        off = pl.multiple_of(gi * SUBLANES, SUBLANES)
        a = a_s[pl.ds(off, SUBLANES), :]
        b = b_s[pl.ds(off, SUBLANES), :]
        for d in (1, 2, 4):
            a_sh = pltpu.roll(a, d, 0)
            b_sh = pltpu.roll(b, d, 0)
            keep = row >= d
            b = jnp.where(keep, a * b_sh + b, b)
            a = jnp.where(keep, a * a_sh, a)
        h = b + a * h_prev
        b_s[pl.ds(off, SUBLANES), :] = h
        return h[SUBLANES - 1:SUBLANES, :]

    carry[...] = lax.fori_loop(0, ts // SUBLANES, group, carry[...])
    gate = cg_ref[...]
    gelu = 0.5 * gate * (1.0 + jnp.tanh(0.7978845608028654 * (gate + 0.044715 * (gate * gate * gate))))
    o_ref[...] = (gelu * b_s[...]).astype(o_ref.dtype)


def rglru(pc, cw, cb, wa, ba, wx, bx, lam, ts=256):
    bsz, s, _ = pc.shape

    def const(shape):
        return pl.BlockSpec(shape, lambda b, i: (0,) * len(shape))

    vec = const((1, C_WIDTH))
    return pl.pallas_call(
        functools.partial(_rglru_kernel, ts=ts),
        grid=(bsz, s // ts),
        in_specs=[pl.BlockSpec((None, ts, C_WIDTH), lambda b, i: (b, i, 0)),
                  pl.BlockSpec((None, ts, C_WIDTH), lambda b, i: (b, i, 1)),
                  const((C_CONV, C_WIDTH)), vec, const(wa.shape), vec, const(wx.shape), vec, vec],
        out_specs=pl.BlockSpec((None, ts, C_WIDTH), lambda b, i: (b, i, 0)),
        out_shape=jax.ShapeDtypeStruct((bsz, s, C_WIDTH), BF16),
        scratch_shapes=[pltpu.VMEM((ts + SUBLANES, C_WIDTH), F32), pltpu.VMEM((ts, C_WIDTH), F32),
                        pltpu.VMEM((ts, C_WIDTH), F32), pltpu.VMEM((1, C_WIDTH), F32)],
        compiler_params=_cparams(("parallel", "arbitrary")),
        name="rglru",
    )(pc, pc, cw, cb, wa, ba, wx, bx, lam)


def _merge_kernel(oc_ref, os_ref, ow_ref, ag_ref, yb_ref, yc_ref, ma_ref, mb_ref, mc_ref, x_ref,
                  wpa_ref, wpb_ref, wpc_ref, wo_ref, o_ref, ya_s):
    gates = jax.nn.sigmoid(ag_ref[...])
    for h in range(A_HEADS):
        sl = slice(h * SLOT, (h + 1) * SLOT)
        y = (gates[:, 3 * h:3 * h + 1] * oc_ref[:, sl].astype(F32)
             + gates[:, 3 * h + 1:3 * h + 2] * os_ref[:, sl].astype(F32)
             + gates[:, 3 * h + 2:3 * h + 3] * ow_ref[:, sl].astype(F32))
        ya_s[:, sl] = y.astype(BF16)
    merged = (jax.nn.sigmoid(ma_ref[...]) * _dot(ya_s[...], wpa_ref[...])
              + jax.nn.sigmoid(mb_ref[...]) * _dot(yb_ref[...], wpb_ref[...])
              + jax.nn.sigmoid(mc_ref[...]) * _dot(yc_ref[...], wpc_ref[...]))
    o_ref[...] = x_ref[...] + _dot(merged.astype(BF16), wo_ref[...])


def merge(oc, osel, ow, pa2d, yb, yc, pm, x2d, wpa, wpb, wpc, wo, tm=256):
    t = x2d.shape[0]
    d = D_MODEL

    def rows(width, off=0):
        return pl.BlockSpec((tm, width), lambda i: (i, off // width))

    w_spec = pl.BlockSpec((d, d), lambda i: (0, 0))
    return pl.pallas_call(
        _merge_kernel,
        grid=(t // tm,),
        in_specs=[rows(d), rows(d), rows(d), rows(SLOT, PA_G), rows(d), rows(d),
                  rows(d, 0), rows(d, d), rows(d, 2 * d), rows(d),
                  w_spec, w_spec, w_spec, w_spec],
        out_specs=rows(d),
        out_shape=jax.ShapeDtypeStruct((t, d), F32),
        scratch_shapes=[pltpu.VMEM((tm, d), BF16)],
        compiler_params=_cparams(("parallel",)),
        name="merge",
    )(oc, osel, ow, pa2d, yb, yc, pm, pm, pm, x2d, wpa, wpb, wpc, wo)


def _ffn_kernel(x_ref, g_ref, w1_ref, w3_ref, w2_ref, o_ref, h_s, acc_s, *, nf):
    f = pl.program_id(1)

    @pl.when(f == 0)
    def _():
        x = x_ref[...]
        inv = lax.rsqrt(jnp.mean(x * x, axis=-1, keepdims=True) + NORM_EPS)
        h_s[...] = ((x * inv) * g_ref[...]).astype(BF16)
        acc_s[...] = jnp.zeros(acc_s.shape, F32)

    h = h_s[...]
    u = _dot(h, w1_ref[...])
    z = (u * jax.nn.sigmoid(u)) * _dot(h, w3_ref[...])
    acc_s[...] += _dot(z.astype(BF16), w2_ref[...])

    @pl.when(f == nf - 1)
    def _():
        o_ref[...] = x_ref[...] + acc_s[...]


def ffn(x2d, g, w1, w3, w2, tm=1024, tf=256):
    t, d = x2d.shape
    hidden = w1.shape[1]
    nf = hidden // tf
    return pl.pallas_call(
        functools.partial(_ffn_kernel, nf=nf),
        grid=(t // tm, nf),
        in_specs=[pl.BlockSpec((tm, d), lambda i, f: (i, 0)),
                  pl.BlockSpec((1, d), lambda i, f: (0, 0)),
                  pl.BlockSpec((d, tf), lambda i, f: (0, f)),
                  pl.BlockSpec((d, tf), lambda i, f: (0, f)),
                  pl.BlockSpec((tf, d), lambda i, f: (f, 0))],
        out_specs=pl.BlockSpec((tm, d), lambda i, f: (i, 0)),
        out_shape=jax.ShapeDtypeStruct((t, d), F32),
        scratch_shapes=[pltpu.VMEM((tm, d), BF16), pltpu.VMEM((tm, d), F32)],
        compiler_params=_cparams(("parallel", "arbitrary")),
        name="ffn",
    )(x2d, g.reshape(1, d), w1, w3, w2)


def _pad_cols(w, width):
    return jnp.pad(w, ((0, 0), (0, width - w.shape[1])))


def _slot_cols(w, heads, real):
    k = w.shape[0]
    return jnp.pad(w.reshape(k, heads, real), ((0, 0), (0, 0), (0, SLOT - real))).reshape(k, heads * SLOT)


def _slot_rows(w, heads, real):
    n = w.shape[1]
    return jnp.pad(w.reshape(heads, real, n), ((0, 0), (0, SLOT - real), (0, 0))).reshape(heads * SLOT, n)


def _split_w_in(w):
    sizes = (A_Q, A_KV, A_KV, A_KV, A_KV, A_KV, A_KV, A_GATES, B_Q_LORA, B_KV_LORA, B_ROPE,
             C_WIDTH, C_WIDTH, D_MODEL, D_MODEL, D_MODEL)
    offs = np.cumsum((0,) + sizes)
    (aq, akc, avc, aks, avs, akw, avw, ag, bcq, bckv, bkpe, cg, cx, ma, mb, mc) = [
        w[:, offs[n]:offs[n + 1]] for n in range(len(sizes))]
    wa = jnp.concatenate([
        _slot_cols(aq, A_HEADS, A_HEAD_DIM), akc, avc,
        _slot_cols(aks, A_GROUPS, A_HEAD_DIM), _slot_cols(avs, A_GROUPS, A_HEAD_DIM),
        _slot_cols(akw, A_GROUPS, A_HEAD_DIM), _slot_cols(avw, A_GROUPS, A_HEAD_DIM),
        _pad_cols(ag, 2 * SLOT)], axis=1)
    kpe_slot = jnp.pad(bkpe, ((0, 0), (B_NOPE, SLOT - B_NOPE - B_ROPE)))
    wb = jnp.concatenate([bckv, kpe_slot, bcq], axis=1)
    wc = jnp.concatenate([cg, cx], axis=1)
    wm = jnp.concatenate([ma, mb, mc], axis=1)
    return [m.astype(BF16) for m in (wa, wb, wc, wm)]


def _rope_tables(positions, half, lo):
    inv = ROPE_THETA ** (-jnp.arange(half, dtype=F32) / half)
    ang = positions.astype(F32)[..., None] * inv
    cos, sin = jnp.cos(ang), jnp.sin(ang)
    shape = ang.shape[:-1]
    ones = jnp.ones(shape + (lo,), F32)
    tail = SLOT - lo - 2 * half
    z = lambda n: jnp.zeros(shape + (n,), F32)
    cos_t = jnp.concatenate([ones, cos, cos, jnp.ones(shape + (tail,), F32)], axis=-1)
    sin_lo = jnp.concatenate([z(lo), -sin, z(half), z(tail)], axis=-1)
    sin_hi = jnp.concatenate([z(lo), z(half), sin, z(tail)], axis=-1)
    return cos_t, sin_lo, sin_hi


def _overlap_t(s):
    n_cmp = s // CMP_STRIDE
    n_sel = s // SEL_LEN
    starts = np.arange(n_cmp) * CMP_STRIDE
    sel_starts = np.arange(n_sel) * SEL_LEN
    ovl = np.clip(np.minimum(starts[:, None] + CMP_LEN, sel_starts[None, :] + SEL_LEN)
                  - np.maximum(starts[:, None], sel_starts[None, :]), 0, None) / CMP_LEN
    ovl[(s - CMP_LEN) // CMP_STRIDE + 1:, :] = 0.0
    return jnp.asarray(ovl.T, dtype=F32)


def _vec_slot(g):
    return jnp.pad(g, (0, SLOT - g.shape[0])).reshape(1, SLOT).astype(F32)


def kernel(x, positions, mix_norm_g, w_in, a_q_norm_g, a_k_norm_g, a_cmp_pos, a_cmp_w1, a_cmp_w2, b_cq_norm_g, b_ckv_norm_g, b_w_uq, b_w_ukv, b_q_norm_g, b_k_norm_g, c_conv_w, c_conv_b, c_w_a, c_b_a, c_w_x, c_b_x, c_lambda, w_pa, w_pb, w_pc, w_o, ffn_norm_g, ffn_w1, ffn_w3, ffn_w2):
    bsz, s, d = x.shape
    t = bsz * s
    depth = w_in.shape[0]
    n_chunk = s // CMP_STRIDE
    tabs_a = _rope_tables(positions, A_HEAD_DIM // 2, 0)
    tabs_b = _rope_tables(positions, B_ROPE // 2, B_NOPE)
    ovl_t = _overlap_t(s)

    x2d = x.reshape(t, d)
    for l in range(depth):
        wa, wb, wc, wm = _split_w_in(w_in[l])
        pa = norm_matmul(x2d, mix_norm_g[l], wa)
        pb = norm_matmul(x2d, mix_norm_g[l], wb)
        pc = norm_matmul(x2d, mix_norm_g[l], wc)
        pm = norm_matmul(x2d, mix_norm_g[l], wm)
        pa3 = pa.reshape(bsz, s, PA_N)

        qg, kg = _vec_slot(a_q_norm_g[l]), _vec_slot(a_k_norm_g[l])
        qn, qr, ks, vs, kw, vw = nsa_prep(pa3, tabs_a, qg, kg)

        def chunks(off):
            u = pa3[:, :, off:off + A_KV].reshape(bsz, n_chunk, CMP_STRIDE, A_GROUPS, A_HEAD_DIM)
            return u.transpose(0, 3, 1, 2, 4).reshape(bsz, A_GROUPS, n_chunk, CMP_STRIDE * A_HEAD_DIM)

        w1 = jnp.pad(a_cmp_w1[l], ((0, 0), (0, 0), (0, SLOT - A_HEAD_DIM))).astype(BF16)
        w2 = jnp.pad(a_cmp_w2[l], ((0, 0), (0, SLOT - A_HEAD_DIM), (0, SLOT - A_HEAD_DIM))).astype(BF16)
        pos = jnp.broadcast_to(a_cmp_pos[l].reshape(2, 1, CMP_LEN * A_HEAD_DIM),
                               (2, SUBLANES, CMP_LEN * A_HEAD_DIM)).astype(BF16)
        kc, vc = nsa_compress(chunks(PA_KC), chunks(PA_VC), w1, w2, pos, kg)
        oc, sel = nsa_cmp_attn(qn, kc, vc, ovl_t)
        osel = flash_attention(qr, ks, vs, sel, hpg=A_HPG, mode="select", tq=256, tk=512)
        ow = flash_attention(qr, kw, vw, hpg=A_HPG, mode="window", tq=256, tk=256)

        pb3 = pb.reshape(bsz, s, PB_N)
        wuq = _slot_cols(b_w_uq[l], B_HEADS, B_QK).astype(BF16)
        ukv = b_w_ukv[l].reshape(B_KV_LORA, B_HEADS, B_NOPE + B_V)
        wuk = _slot_cols(ukv[:, :, :B_NOPE].reshape(B_KV_LORA, B_HEADS * B_NOPE), B_HEADS, B_NOPE).astype(BF16)
        wuv = _slot_cols(ukv[:, :, B_NOPE:].reshape(B_KV_LORA, B_HEADS * B_V), B_HEADS, B_V).astype(BF16)
        bq, bk, bv = mla_prep(pb3, b_cq_norm_g[l].reshape(1, -1), b_ckv_norm_g[l].reshape(1, -1),
                              wuq, wuk, wuv, tabs_b, _vec_slot(b_q_norm_g[l]), _vec_slot(b_k_norm_g[l]))
        yb = flash_attention(bq, bk, bv, hpg=1, mode="causal", tq=512, tk=512)

        yc = rglru(pc.reshape(bsz, s, 2 * C_WIDTH), c_conv_w[l], c_conv_b[l].reshape(1, -1),
                   c_w_a[l].astype(BF16), c_b_a[l].reshape(1, -1), c_w_x[l].astype(BF16),
                   c_b_x[l].reshape(1, -1), c_lambda[l].reshape(1, -1))

        x2d = merge(oc.reshape(t, -1), osel.reshape(t, -1), ow.reshape(t, -1), pa,
                    yb.reshape(t, -1), yc.reshape(t, -1), pm, x2d,
                    _slot_rows(w_pa[l], A_HEADS, A_HEAD_DIM).astype(BF16),
                    _slot_rows(w_pb[l], B_HEADS, B_V).astype(BF16),
                    w_pc[l].astype(BF16), w_o[l].astype(BF16))
        x2d = ffn(x2d, ffn_norm_g[l], ffn_w1[l].astype(BF16), ffn_w3[l].astype(BF16), ffn_w2[l].astype(BF16))
    return x2d.reshape(bsz, s, d)
```

```python
import functools

import numpy as np
import jax
import jax.numpy as jnp
from jax import lax
from jax.experimental import pallas as pl
from jax.experimental.pallas import tpu as pltpu

F32 = jnp.float32
BF16 = jnp.bfloat16

D_MODEL = 1024
ROPE_THETA = 10000.0
NORM_EPS = 1e-6
NEG_INF = -1e30
LOG2E = 1.4426950408889634

A_HEADS = 8
A_GROUPS = 2
A_HPG = A_HEADS // A_GROUPS
A_HEAD_DIM = 64
CMP_LEN = 32
CMP_STRIDE = 16
SEL_LEN = 64
SEL_TOPK = 16
WINDOW = 512

B_HEADS = 8
B_Q_LORA = 384
B_KV_LORA = 256
B_NOPE = 64
B_ROPE = 32
B_V = 64
B_QK = B_NOPE + B_ROPE

C_WIDTH = 1024
C_BLOCKS = 8
C_BLOCK_W = C_WIDTH // C_BLOCKS
C_CONV = 4
C_SCALE = 8.0

FFN_HIDDEN = 2816

A_Q = A_HEADS * A_HEAD_DIM
A_KV = A_GROUPS * A_HEAD_DIM
A_GATES = 3 * A_HEADS

LANE = 128
SLOT = LANE
VMEM_LIMIT = 56 * 1024 * 1024

PA_Q = 0
PA_KC = A_HEADS * SLOT
PA_VC = PA_KC + A_KV
PA_KS = PA_VC + A_KV
PA_VS = PA_KS + A_GROUPS * SLOT
PA_KW = PA_VS + A_GROUPS * SLOT
PA_VW = PA_KW + A_GROUPS * SLOT
PA_G = PA_VW + A_GROUPS * SLOT
PA_N = PA_G + 2 * SLOT

PB_KV = 0
PB_PE = B_KV_LORA
PB_Q = PB_PE + SLOT
PB_N = PB_Q + B_Q_LORA


def _cparams(sem):
    return pltpu.CompilerParams(dimension_semantics=sem, vmem_limit_bytes=VMEM_LIMIT)


def _dot(a, b):
    return jnp.dot(a, b, preferred_element_type=F32)


def _dot_nt(a, b):
    return lax.dot_general(a, b, (((1,), (1,)), ((), ())), preferred_element_type=F32)


def _norm_matmul_kernel(x_ref, g_ref, w_ref, o_ref, h_ref):
    @pl.when(pl.program_id(1) == 0)
    def _():
        x = x_ref[...]
        inv = lax.rsqrt(jnp.mean(x * x, axis=-1, keepdims=True) + NORM_EPS)
        h_ref[...] = ((x * inv) * g_ref[...]).astype(BF16)

    o_ref[...] = _dot(h_ref[...], w_ref[...])


def _pick_tn(n):
    if n <= 1024:
        return n
    for tn in (1280, 1024, 768, 640, 512, 384, 256, 128):
        if n % tn == 0:
            return tn
    raise ValueError(n)


def norm_matmul(x2d, g, w, tm=1024):
    t, k = x2d.shape
    n = w.shape[1]
    tn = _pick_tn(n)
    return pl.pallas_call(
        _norm_matmul_kernel,
        grid=(t // tm, n // tn),
        in_specs=[pl.BlockSpec((tm, k), lambda i, j: (i, 0)),
                  pl.BlockSpec((1, k), lambda i, j: (0, 0)),
                  pl.BlockSpec((k, tn), lambda i, j: (0, j))],
        out_specs=pl.BlockSpec((tm, tn), lambda i, j: (i, j)),
        out_shape=jax.ShapeDtypeStruct((t, n), F32),
        scratch_shapes=[pltpu.VMEM((tm, k), BF16)],
        compiler_params=_cparams(("parallel", "arbitrary")),
        name="norm_matmul",
    )(x2d, g.reshape(1, k), w)


def _slot_rms(x, g, width):
    ms = jnp.sum(x * x, axis=-1, keepdims=True) * (1.0 / width)
    return (x * lax.rsqrt(ms + NORM_EPS)) * g


def _slot_rope(y, cos, sin_lo, sin_hi, half):
    return y * cos + pltpu.roll(y, LANE - half, 1) * sin_lo + pltpu.roll(y, half, 1) * sin_hi


def _value_t(v, real):
    vt = v.T
    row = lax.broadcasted_iota(jnp.int32, vt.shape, 0)
    return jnp.where(row == real, 1.0, vt).astype(BF16)


def _nsa_prep_kernel(aq_ref, ks_ref, vs_ref, kw_ref, vw_ref, cos_ref, slo_ref, shi_ref, qg_ref, kg_ref,
                     qn_o, qr_o, ks_o, vs_o, kw_o, vw_o, *, tm):
    cos, slo, shi = cos_ref[...], slo_ref[...], shi_ref[...]
    qg, kg = qg_ref[...], kg_ref[...]
    scale = A_HEAD_DIM ** -0.5
    half = A_HEAD_DIM // 2
    for h in range(A_HEADS):
        sl = slice(h * SLOT, (h + 1) * SLOT)
        y = _slot_rms(aq_ref[:, sl], qg, A_HEAD_DIM)
        qn_o[:, sl] = (y * scale).astype(BF16)
        qr_o[:, sl] = (_slot_rope(y, cos, slo, shi, half) * (scale * LOG2E)).astype(BF16)
    t = pl.program_id(1) * tm + lax.broadcasted_iota(jnp.int32, (tm, SLOT), 0)
    lane = lax.broadcasted_iota(jnp.int32, (tm, SLOT), 1)
    blk_bias = jnp.where(lane == A_HEAD_DIM + t // SEL_LEN, NEG_INF, 0.0)
    for g in range(A_GROUPS):
        sl = slice(g * SLOT, (g + 1) * SLOT)
        ksel = _slot_rope(_slot_rms(ks_ref[:, sl], kg, A_HEAD_DIM), cos, slo, shi, half)
        ks_o[:, sl] = (ksel + blk_bias).astype(BF16)
        kw_o[:, sl] = _slot_rope(_slot_rms(kw_ref[:, sl], kg, A_HEAD_DIM), cos, slo, shi, half).astype(BF16)
        vs_o[sl, :] = _value_t(vs_ref[:, sl], A_HEAD_DIM)
        vw_o[sl, :] = _value_t(vw_ref[:, sl], A_HEAD_DIM)


def nsa_prep(pa, tabs, qg, kg, tm=512):
    bsz, s, _ = pa.shape
    gw = A_GROUPS * SLOT
    assert s // SEL_LEN <= SLOT - A_HEAD_DIM, "selection blocks must fit the free lanes of a head slot"

    def col(width, off):
        return pl.BlockSpec((None, tm, width), lambda b, i: (b, i, off // width))

    tab = pl.BlockSpec((None, tm, LANE), lambda b, i: (b, i, 0))
    vec = pl.BlockSpec((1, LANE), lambda b, i: (0, 0))
    out_q = pl.BlockSpec((None, tm, A_HEADS * SLOT), lambda b, i: (b, i, 0))
    out_k = pl.BlockSpec((None, tm, gw), lambda b, i: (b, i, 0))
    out_vt = pl.BlockSpec((None, gw, tm), lambda b, i: (b, 0, i))
    sq = jax.ShapeDtypeStruct((bsz, s, A_HEADS * SLOT), BF16)
    sk = jax.ShapeDtypeStruct((bsz, s, gw), BF16)
    svt = jax.ShapeDtypeStruct((bsz, gw, s), BF16)
    return pl.pallas_call(
        functools.partial(_nsa_prep_kernel, tm=tm),
        grid=(bsz, s // tm),
        in_specs=[col(A_HEADS * SLOT, PA_Q), col(gw, PA_KS), col(gw, PA_VS), col(gw, PA_KW), col(gw, PA_VW),
                  tab, tab, tab, vec, vec],
        out_specs=[out_q, out_q, out_k, out_vt, out_k, out_vt],
        out_shape=[sq, sq, sk, svt, sk, svt],
        compiler_params=_cparams(("parallel", "parallel")),
        name="nsa_prep",
    )(pa, pa, pa, pa, pa, *tabs, qg, kg)


def _compress_kernel(uk_ref, uv_ref, w1_ref, w2_ref, pos_ref, kg_ref, kc_o, vc_o):
    half = CMP_STRIDE * A_HEAD_DIM
    n = uk_ref.shape[0]
    for z, (u_ref, o_ref) in enumerate(((uk_ref, kc_o), (uv_ref, vc_o))):
        u = u_ref[...].astype(BF16)
        first = _dot(u, w1_ref[z, :half, :])
        second = _dot(u, w1_ref[z, half:, :])
        posc = _dot(pos_ref[z], w1_ref[z])[0:1, :]
        hid = first + pltpu.roll(second, n - 1, 0) + posc
        hid = hid * jax.nn.sigmoid(hid)
        comp = _dot(hid.astype(BF16), w2_ref[z])
        if z == 0:
            comp = _slot_rms(comp, kg_ref[...], A_HEAD_DIM)
        o_ref[...] = comp.astype(BF16)


def nsa_compress(uk, uv, w1, w2, pos, kg):
    bsz, g, n, f = uk.shape
    u_spec = pl.BlockSpec((None, None, n, f), lambda b, gi: (b, gi, 0, 0))
    o_spec = pl.BlockSpec((None, None, n, SLOT), lambda b, gi: (b, gi, 0, 0))
    so = jax.ShapeDtypeStruct((bsz, g, n, SLOT), BF16)
    return pl.pallas_call(
        _compress_kernel,
        grid=(bsz, g),
        in_specs=[u_spec, u_spec,
                  pl.BlockSpec(w1.shape, lambda b, gi: (0, 0, 0)),
                  pl.BlockSpec(w2.shape, lambda b, gi: (0, 0, 0)),
                  pl.BlockSpec(pos.shape, lambda b, gi: (0, 0, 0)),
                  pl.BlockSpec((1, LANE), lambda b, gi: (0, 0))],
        out_specs=[o_spec, o_spec],
        out_shape=[so, so],
        compiler_params=_cparams(("parallel", "parallel")),
        name="nsa_compress",
    )(uk, uv, w1, w2, pos, kg)


def _cmp_attn_kernel(q_ref, kc_ref, vc_ref, ovl_ref, oc_o, sel_o, *, tq):
    t0 = pl.program_id(2) * tq
    kc = kc_ref[...]
    vc = vc_ref[...]
    n_cmp = kc.shape[0]
    n_sel = ovl_ref.shape[0]
    t = t0 + lax.broadcasted_iota(jnp.int32, (tq, n_cmp), 0)
    n = lax.broadcasted_iota(jnp.int32, (tq, n_cmp), 1)
    mask = t >= n * CMP_STRIDE + (CMP_LEN - 1)
    psum = jnp.zeros((tq, n_cmp), F32)
    for h in range(A_HPG):
        sl = slice(h * SLOT, (h + 1) * SLOT)
        s = jnp.where(mask, _dot_nt(q_ref[:, sl], kc), NEG_INF)
        m = jnp.max(s, axis=-1, keepdims=True)
        p = jnp.where(mask, jnp.exp(s - m), 0.0)
        l = jnp.sum(p, axis=-1, keepdims=True)
        p = p / jnp.where(l > 0.0, l, 1.0)
        psum = psum + p
        oc_o[:, sl] = _dot(p.astype(BF16), vc).astype(oc_o.dtype)
    imp = lax.dot_general(ovl_ref[...], psum, (((1,), (1,)), ((), ())),
                          precision=lax.Precision.HIGHEST, preferred_element_type=F32)
    j = lax.broadcasted_iota(jnp.int32, (n_sel, tq), 0)
    cur = (t0 + lax.broadcasted_iota(jnp.int32, (n_sel, tq), 1)) // SEL_LEN
    big = -NEG_INF
    imp = jnp.where(j == 0, 3.0 * big, jnp.where(j == cur, 2.0 * big, jnp.where(j == cur - 1, big, imp)))
    imp = jnp.where(j > cur, NEG_INF, imp)
    work = imp
    kth = None
    for _ in range(min(SEL_TOPK, n_sel)):
        kth = jnp.max(work, axis=0, keepdims=True)
        work = jnp.where(work >= kth, -jnp.inf, work)
    notsel = jnp.where(jnp.logical_and(imp >= kth, j <= cur), 0.0, 1.0)
    free = SLOT - A_HEAD_DIM
    parts = [jnp.zeros((A_HEAD_DIM, tq), F32), notsel]
    if n_sel < free:
        parts.append(jnp.zeros((free - n_sel, tq), F32))
    sel_o[...] = jnp.concatenate(parts, axis=0).T.astype(sel_o.dtype)


def nsa_cmp_attn(qn, kc, vc, ovl_t, tq=256):
    bsz, s, _ = qn.shape
    n_cmp = kc.shape[2]
    n_sel = ovl_t.shape[0]
    gq = A_HPG * SLOT
    kv_spec = pl.BlockSpec((None, None, n_cmp, SLOT), lambda b, g, i: (b, g, 0, 0))
    return pl.pallas_call(
        functools.partial(_cmp_attn_kernel, tq=tq),
        grid=(bsz, A_GROUPS, s // tq),
        in_specs=[pl.BlockSpec((None, tq, gq), lambda b, g, i: (b, i, g)),
                  kv_spec, kv_spec,
                  pl.BlockSpec(ovl_t.shape, lambda b, g, i: (0, 0))],
        out_specs=[pl.BlockSpec((None, tq, gq), lambda b, g, i: (b, i, g)),
                   pl.BlockSpec((None, None, tq, SLOT), lambda b, g, i: (b, g, i, 0))],
        out_shape=[jax.ShapeDtypeStruct((bsz, s, A_HEADS * SLOT), BF16),
                   jax.ShapeDtypeStruct((bsz, A_GROUPS, s, SLOT), BF16)],
        compiler_params=_cparams(("parallel", "parallel", "parallel")),
        name="nsa_cmp_attn",
    )(qn, kc, vc, ovl_t)


def _flash_kernel(*refs, hb, hpg, real_v, mode, tq, tk, nk, back):
    if mode == "select":
        q_ref, k_ref, vt_ref, nsel_ref, o_ref, qst_s, m_s, acc_s = refs
    else:
        q_ref, k_ref, vt_ref, o_ref, qst_s, m_s, acc_s = refs
    i = pl.program_id(2)
    j = pl.program_id(3)
    q0 = i * tq
    n = hpg * tq
    kt = i * (tq // tk) - back + j if mode == "window" else j
    k0 = kt * tk

    @pl.when(j == 0)
    def _():
        for b in range(hb):
            for h in range(hpg):
                qh = q_ref[:, (b * hpg + h) * SLOT:(b * hpg + h + 1) * SLOT]
                if mode == "select":
                    qh = qh + nsel_ref[b]
                qst_s[b, h * tq:(h + 1) * tq, :] = qh
        m_s[...] = jnp.full(m_s.shape, NEG_INF, F32)
        acc_s[...] = jnp.zeros(acc_s.shape, F32)

    def step(kind):
        if kind != "none":
            key = k0 + lax.broadcasted_iota(jnp.int32, (tk, n), 0)
            qry = q0 + (lax.broadcasted_iota(jnp.int32, (tk, n), 1) & (tq - 1))
            visible = key <= qry if kind == "causal" else key > qry - WINDOW
        for b in range(hb):
            s = _dot_nt(k_ref[:, b * SLOT:(b + 1) * SLOT], qst_s[b])
            if kind != "none":
                s = jnp.where(visible, s, NEG_INF)
            m_prev = m_s[b]
            m_new = jnp.maximum(m_prev, jnp.max(s, axis=0, keepdims=True))
            alpha = jnp.exp2(m_prev - m_new)
            p = jnp.exp2(s - m_new).astype(BF16)
            acc_s[b] = alpha * acc_s[b] + _dot(vt_ref[b * SLOT:(b + 1) * SLOT, :], p)
            m_s[b] = m_new

    if mode == "window":
        @pl.when(jnp.logical_and(j == 0, kt >= 0))
        def _():
            step("lower")

        @pl.when(jnp.logical_and(jnp.logical_and(j > 0, j < nk - 1), kt >= 0))
        def _():
            step("none")

        @pl.when(j == nk - 1)
        def _():
            step("causal")
    else:
        @pl.when(k0 + (tk - 1) <= q0)
        def _():
            step("none")

        @pl.when(jnp.logical_and(k0 + (tk - 1) > q0, k0 <= q0 + (tq - 1)))
        def _():
            step("causal")

    @pl.when(j == nk - 1)
    def _():
        for b in range(hb):
            acc = acc_s[b]
            out = acc / acc[real_v:real_v + 1, :]
            for h in range(hpg):
                o_ref[:, (b * hpg + h) * SLOT:(b * hpg + h + 1) * SLOT] = (
                    out[:, h * tq:(h + 1) * tq].T.astype(o_ref.dtype))


def flash_attention(q, k, vt, nsel=None, *, hb, hpg, real_v, mode, tq, tk):
    bsz, s, qw = q.shape
    groups = qw // (hpg * SLOT)
    assert tq & (tq - 1) == 0 and groups % hb == 0
    if mode == "window":
        assert tq == tk and WINDOW % tk == 0
        back = WINDOW // tk
        nk = back + 1

        def kv_tile(i, j):
            return jnp.maximum(i - back + j, 0)
    else:
        back = 0
        nk = s // tk

        def kv_tile(i, j):
            return jnp.minimum(j, (i * tq + tq - 1) // tk)

    in_specs = [pl.BlockSpec((None, tq, hb * hpg * SLOT), lambda b, g, i, j: (b, i, g)),
                pl.BlockSpec((None, tk, hb * SLOT), lambda b, g, i, j: (b, kv_tile(i, j), g)),
                pl.BlockSpec((None, hb * SLOT, tk), lambda b, g, i, j: (b, g, kv_tile(i, j)))]
    args = [q, k, vt]
    if mode == "select":
        in_specs.append(pl.BlockSpec((None, hb, tq, SLOT), lambda b, g, i, j: (b, g, i, 0)))
        args.append(nsel)
    return pl.pallas_call(
        functools.partial(_flash_kernel, hb=hb, hpg=hpg, real_v=real_v, mode=mode, tq=tq, tk=tk, nk=nk, back=back),
        grid=(bsz, groups // hb, s // tq, nk),
        in_specs=in_specs,
        out_specs=pl.BlockSpec((None, tq, hb * hpg * SLOT), lambda b, g, i, j: (b, i, g)),
        out_shape=jax.ShapeDtypeStruct((bsz, s, qw), BF16),
        scratch_shapes=[pltpu.VMEM((hb, hpg * tq, SLOT), BF16), pltpu.VMEM((hb, 1, hpg * tq), F32),
                        pltpu.VMEM((hb, SLOT, hpg * tq), F32)],
        compiler_params=_cparams(("parallel", "parallel", "parallel", "arbitrary")),
        name="flash_" + mode,
    )(*args)


def _mla_prep_kernel(ckv_ref, kpe_ref, cq_ref, cqg_ref, ckvg_ref, wuq_ref, wuk_ref, wuv_ref,
                     cos_ref, slo_ref, shi_ref, qg_ref, kg_ref, q_o, k_o, v_o):
    def rms(x, g):
        return ((x * lax.rsqrt(jnp.mean(x * x, axis=-1, keepdims=True) + NORM_EPS)) * g).astype(BF16)

    cq = rms(cq_ref[...], cqg_ref[...])
    ckv = rms(ckv_ref[...], ckvg_ref[...])
    q = _dot(cq, wuq_ref[...])
    kn = _dot(ckv, wuk_ref[...])
    vv = _dot(ckv, wuv_ref[...])
    kpe = kpe_ref[...]
    cos, slo, shi = cos_ref[...], slo_ref[...], shi_ref[...]
    qg, kg = qg_ref[...], kg_ref[...]
    scale = B_QK ** -0.5
    half = B_ROPE // 2
    for h in range(B_HEADS):
        sl = slice(h * SLOT, (h + 1) * SLOT)
        yq = _slot_rope(_slot_rms(q[:, sl], qg, B_QK), cos, slo, shi, half)
        q_o[:, sl] = (yq * (scale * LOG2E)).astype(BF16)
        yk = _slot_rope(_slot_rms(kn[:, sl] + kpe, kg, B_QK), cos, slo, shi, half)
        k_o[:, sl] = yk.astype(BF16)
        v_o[sl, :] = _value_t(vv[:, sl], B_V)


def mla_prep(pb, cqg, ckvg, wuq, wuk, wuv, tabs, qg, kg, tm=256):
    bsz, s, _ = pb.shape
    hw = B_HEADS * SLOT

    def const(shape):
        return pl.BlockSpec(shape, lambda b, i: (0,) * len(shape))

    tab = pl.BlockSpec((None, tm, LANE), lambda b, i: (b, i, 0))
    out = pl.BlockSpec((None, tm, hw), lambda b, i: (b, i, 0))
    so = jax.ShapeDtypeStruct((bsz, s, hw), BF16)
    return pl.pallas_call(
        _mla_prep_kernel,
        grid=(bsz, s // tm),
        in_specs=[pl.BlockSpec((None, tm, B_KV_LORA), lambda b, i: (b, i, PB_KV // B_KV_LORA)),
                  pl.BlockSpec((None, tm, SLOT), lambda b, i: (b, i, PB_PE // SLOT)),
                  pl.BlockSpec((None, tm, B_Q_LORA), lambda b, i: (b, i, PB_Q // B_Q_LORA)),
                  const((1, B_Q_LORA)), const((1, B_KV_LORA)),
                  const(wuq.shape), const(wuk.shape), const(wuv.shape),
                  tab, tab, tab, const((1, LANE)), const((1, LANE))],
        out_specs=[out, out, pl.BlockSpec((None, hw, tm), lambda b, i: (b, 0, i))],
        out_shape=[so, so, jax.ShapeDtypeStruct((bsz, hw, s), BF16)],
        compiler_params=_cparams(("parallel", "parallel")),
        name="mla_prep",
    )(pb, pb, pb, cqg, ckvg, wuq, wuk, wuv, *tabs, qg, kg)


SUBLANES = 8


def _rglru_kernel(cg_ref, cx_ref, cw_ref, cb_ref, wa_ref, ba_ref, wx_ref, bx_ref, lam_ref, o_ref,
                  xbuf, a_s, b_s, carry, *, ts):
    @pl.when(pl.program_id(1) == 0)
    def _():
        xbuf[0:SUBLANES, :] = jnp.zeros((SUBLANES, C_WIDTH), F32)
        carry[...] = jnp.zeros(carry.shape, F32)

    xr = cx_ref[...]
    xbuf[SUBLANES:SUBLANES + ts, :] = xr
    xc = cb_ref[...] + xr * cw_ref[C_CONV - 1:C_CONV, :]
    for w in range(C_CONV - 1):
        back = C_CONV - 1 - w
        xc = xc + xbuf[pl.ds(SUBLANES - back, ts), :] * cw_ref[w:w + 1, :]
    xbuf[0:SUBLANES, :] = xr[ts - SUBLANES:ts, :]

    for n in range(C_BLOCKS):
        sl = slice(n * C_BLOCK_W, (n + 1) * C_BLOCK_W)
        xb = xc[:, sl].astype(BF16)
        a_s[:, sl] = _dot(xb, wa_ref[n])
        b_s[:, sl] = _dot(xb, wx_ref[n])
    r = jax.nn.sigmoid(a_s[...] + ba_ref[...])
    ig = jax.nn.sigmoid(b_s[...] + bx_ref[...])
    nlam = -lam_ref[...]
    softplus = jnp.maximum(nlam, 0.0) + jnp.log1p(jnp.exp(-jnp.abs(nlam)))
    log_a = (-C_SCALE) * r * softplus
    a_s[...] = jnp.exp(log_a)
    th = jnp.tanh(log_a)
    b_s[...] = jnp.sqrt((-2.0 * th) / (1.0 - th)) * (ig * xc)

    row = lax.broadcasted_iota(jnp.int32, (SUBLANES, C_WIDTH), 0)

    def group(gi, h_prev):
        off = pl.multiple_of(gi * SUBLANES, SUBLANES)
        a = a_s[pl.ds(off, SUBLANES), :]
        b = b_s[pl.ds(off, SUBLANES), :]
        for d in (1, 2, 4):
            a_sh = pltpu.roll(a, d, 0)
            b_sh = pltpu.roll(b, d, 0)
            keep = row >= d
            b = jnp.where(keep, a * b_sh + b, b)
            a = jnp.where(keep, a * a_sh, a)
        h = b + a * h_prev
        b_s[pl.ds(off, SUBLANES), :] = h
        return h[SUBLANES - 1:SUBLANES, :]

    carry[...] = lax.fori_loop(0, ts // SUBLANES, group, carry[...])
    gate = cg_ref[...]
    gelu = 0.5 * gate * (1.0 + jnp.tanh(0.7978845608028654 * (gate + 0.044715 * (gate * gate * gate))))
    o_ref[...] = (gelu * b_s[...]).astype(o_ref.dtype)


def rglru(pc, cw, cb, wa, ba, wx, bx, lam, ts=256):
    bsz, s, _ = pc.shape

    def const(shape):
        return pl.BlockSpec(shape, lambda b, i: (0,) * len(shape))

    vec = const((1, C_WIDTH))
    return pl.pallas_call(
        functools.partial(_rglru_kernel, ts=ts),
        grid=(bsz, s // ts),
        in_specs=[pl.BlockSpec((None, ts, C_WIDTH), lambda b, i: (b, i, 0)),
                  pl.BlockSpec((None, ts, C_WIDTH), lambda b, i: (b, i, 1)),
                  const((C_CONV, C_WIDTH)), vec, const(wa.shape), vec, const(wx.shape), vec, vec],
        out_specs=pl.BlockSpec((None, ts, C_WIDTH), lambda b, i: (b, i, 0)),
        out_shape=jax.ShapeDtypeStruct((bsz, s, C_WIDTH), BF16),
        scratch_shapes=[pltpu.VMEM((ts + SUBLANES, C_WIDTH), F32), pltpu.VMEM((ts, C_WIDTH), F32),
                        pltpu.VMEM((ts, C_WIDTH), F32), pltpu.VMEM((1, C_WIDTH), F32)],
        compiler_params=_cparams(("parallel", "arbitrary")),
        name="rglru",
    )(pc, pc, cw, cb, wa, ba, wx, bx, lam)


def _merge_kernel(oc_ref, os_ref, ow_ref, ag_ref, yb_ref, yc_ref, ma_ref, mb_ref, mc_ref, x_ref,
                  wpa_ref, wpb_ref, wpc_ref, wo_ref, o_ref, ya_s):
    gates = jax.nn.sigmoid(ag_ref[...])
    for h in range(A_HEADS):
        sl = slice(h * SLOT, (h + 1) * SLOT)
        y = (gates[:, 3 * h:3 * h + 1] * oc_ref[:, sl].astype(F32)
             + gates[:, 3 * h + 1:3 * h + 2] * os_ref[:, sl].astype(F32)
             + gates[:, 3 * h + 2:3 * h + 3] * ow_ref[:, sl].astype(F32))
        ya_s[:, sl] = y.astype(BF16)
    merged = (jax.nn.sigmoid(ma_ref[...]) * _dot(ya_s[...], wpa_ref[...])
              + jax.nn.sigmoid(mb_ref[...]) * _dot(yb_ref[...], wpb_ref[...])
              + jax.nn.sigmoid(mc_ref[...]) * _dot(yc_ref[...], wpc_ref[...]))
    o_ref[...] = x_ref[...] + _dot(merged.astype(BF16), wo_ref[...])


def merge(oc, osel, ow, pa2d, yb, yc, pm, x2d, wpa, wpb, wpc, wo, tm=256):
    t = x2d.shape[0]
    d = D_MODEL

    def rows(width, off=0):
        return pl.BlockSpec((tm, width), lambda i: (i, off // width))

    w_spec = pl.BlockSpec((d, d), lambda i: (0, 0))
    return pl.pallas_call(
        _merge_kernel,
        grid=(t // tm,),
        in_specs=[rows(d), rows(d), rows(d), rows(SLOT, PA_G), rows(d), rows(d),
                  rows(d, 0), rows(d, d), rows(d, 2 * d), rows(d),
                  w_spec, w_spec, w_spec, w_spec],
        out_specs=rows(d),
        out_shape=jax.ShapeDtypeStruct((t, d), F32),
        scratch_shapes=[pltpu.VMEM((tm, d), BF16)],
        compiler_params=_cparams(("parallel",)),
        name="merge",
    )(oc, osel, ow, pa2d, yb, yc, pm, pm, pm, x2d, wpa, wpb, wpc, wo)


def _ffn_kernel(x_ref, g_ref, w1_ref, w3_ref, w2_ref, o_ref, h_s, acc_s, *, nf):
    f = pl.program_id(1)

    @pl.when(f == 0)
    def _():
        x = x_ref[...]
        inv = lax.rsqrt(jnp.mean(x * x, axis=-1, keepdims=True) + NORM_EPS)
        h_s[...] = ((x * inv) * g_ref[...]).astype(BF16)
        acc_s[...] = jnp.zeros(acc_s.shape, F32)

    h = h_s[...]
    u = _dot(h, w1_ref[...])
    z = (u * jax.nn.sigmoid(u)) * _dot(h, w3_ref[...])
    acc_s[...] += _dot(z.astype(BF16), w2_ref[...])

    @pl.when(f == nf - 1)
    def _():
        o_ref[...] = x_ref[...] + acc_s[...]


def ffn(x2d, g, w1, w3, w2, tm=1024, tf=256):
    t, d = x2d.shape
    hidden = w1.shape[1]
    nf = hidden // tf
    return pl.pallas_call(
        functools.partial(_ffn_kernel, nf=nf),
        grid=(t // tm, nf),
        in_specs=[pl.BlockSpec((tm, d), lambda i, f: (i, 0)),
                  pl.BlockSpec((1, d), lambda i, f: (0, 0)),
                  pl.BlockSpec((d, tf), lambda i, f: (0, f)),
                  pl.BlockSpec((d, tf), lambda i, f: (0, f)),
                  pl.BlockSpec((tf, d), lambda i, f: (f, 0))],
        out_specs=pl.BlockSpec((tm, d), lambda i, f: (i, 0)),
        out_shape=jax.ShapeDtypeStruct((t, d), F32),
        scratch_shapes=[pltpu.VMEM((tm, d), BF16), pltpu.VMEM((tm, d), F32)],
        compiler_params=_cparams(("parallel", "arbitrary")),
        name="ffn",
    )(x2d, g.reshape(1, d), w1, w3, w2)


def _pad_cols(w, width):
    return jnp.pad(w, ((0, 0), (0, width - w.shape[1])))


def _slot_cols(w, heads, real):
    k = w.shape[0]
    return jnp.pad(w.reshape(k, heads, real), ((0, 0), (0, 0), (0, SLOT - real))).reshape(k, heads * SLOT)


def _slot_rows(w, heads, real):
    n = w.shape[1]
    return jnp.pad(w.reshape(heads, real, n), ((0, 0), (0, SLOT - real), (0, 0))).reshape(heads * SLOT, n)


def _split_w_in(w):
    sizes = (A_Q, A_KV, A_KV, A_KV, A_KV, A_KV, A_KV, A_GATES, B_Q_LORA, B_KV_LORA, B_ROPE,
             C_WIDTH, C_WIDTH, D_MODEL, D_MODEL, D_MODEL)
    offs = np.cumsum((0,) + sizes)
    (aq, akc, avc, aks, avs, akw, avw, ag, bcq, bckv, bkpe, cg, cx, ma, mb, mc) = [
        w[:, offs[n]:offs[n + 1]] for n in range(len(sizes))]
    wa = jnp.concatenate([
        _slot_cols(aq, A_HEADS, A_HEAD_DIM), akc, avc,
        _slot_cols(aks, A_GROUPS, A_HEAD_DIM), _slot_cols(avs, A_GROUPS, A_HEAD_DIM),
        _slot_cols(akw, A_GROUPS, A_HEAD_DIM), _slot_cols(avw, A_GROUPS, A_HEAD_DIM),
        _pad_cols(ag, 2 * SLOT)], axis=1)
    kpe_slot = jnp.pad(bkpe, ((0, 0), (B_NOPE, SLOT - B_NOPE - B_ROPE)))
    wb = jnp.concatenate([bckv, kpe_slot, bcq], axis=1)
    wc = jnp.concatenate([cg, cx], axis=1)
    wm = jnp.concatenate([ma, mb, mc], axis=1)
    return [m.astype(BF16) for m in (wa, wb, wc, wm)]


def _rope_tables(positions, half, lo):
    inv = ROPE_THETA ** (-jnp.arange(half, dtype=F32) / half)
    ang = positions.astype(F32)[..., None] * inv
    cos, sin = jnp.cos(ang), jnp.sin(ang)
    shape = ang.shape[:-1]
    ones = jnp.ones(shape + (lo,), F32)
    tail = SLOT - lo - 2 * half
    z = lambda n: jnp.zeros(shape + (n,), F32)
    cos_t = jnp.concatenate([ones, cos, cos, jnp.ones(shape + (tail,), F32)], axis=-1)
    sin_lo = jnp.concatenate([z(lo), -sin, z(half), z(tail)], axis=-1)
    sin_hi = jnp.concatenate([z(lo), z(half), sin, z(tail)], axis=-1)
    return cos_t, sin_lo, sin_hi


def _overlap_t(s):
    n_cmp = s // CMP_STRIDE
    n_sel = s // SEL_LEN
    starts = np.arange(n_cmp) * CMP_STRIDE
    sel_starts = np.arange(n_sel) * SEL_LEN
    ovl = np.clip(np.minimum(starts[:, None] + CMP_LEN, sel_starts[None, :] + SEL_LEN)
                  - np.maximum(starts[:, None], sel_starts[None, :]), 0, None) / CMP_LEN
    ovl[(s - CMP_LEN) // CMP_STRIDE + 1:, :] = 0.0
    return jnp.asarray(ovl.T, dtype=F32)


def _vec_slot(g):
    return jnp.pad(g, (0, SLOT - g.shape[0])).reshape(1, SLOT).astype(F32)


def kernel(x, positions, mix_norm_g, w_in, a_q_norm_g, a_k_norm_g, a_cmp_pos, a_cmp_w1, a_cmp_w2, b_cq_norm_g, b_ckv_norm_g, b_w_uq, b_w_ukv, b_q_norm_g, b_k_norm_g, c_conv_w, c_conv_b, c_w_a, c_b_a, c_w_x, c_b_x, c_lambda, w_pa, w_pb, w_pc, w_o, ffn_norm_g, ffn_w1, ffn_w3, ffn_w2):
    bsz, s, d = x.shape
    t = bsz * s
    depth = w_in.shape[0]
    n_chunk = s // CMP_STRIDE
    tabs_a = _rope_tables(positions, A_HEAD_DIM // 2, 0)
    tabs_b = _rope_tables(positions, B_ROPE // 2, B_NOPE)
    ovl_t = _overlap_t(s)

    x2d = x.reshape(t, d)
    for l in range(depth):
        wa, wb, wc, wm = _split_w_in(w_in[l])
        pa = norm_matmul(x2d, mix_norm_g[l], wa)
        pb = norm_matmul(x2d, mix_norm_g[l], wb)
        pc = norm_matmul(x2d, mix_norm_g[l], wc)
        pm = norm_matmul(x2d, mix_norm_g[l], wm)
        pa3 = pa.reshape(bsz, s, PA_N)

        qg, kg = _vec_slot(a_q_norm_g[l]), _vec_slot(a_k_norm_g[l])
        qn, qr, ks, vs, kw, vw = nsa_prep(pa3, tabs_a, qg, kg)

        def chunks(off):
            u = pa3[:, :, off:off + A_KV].reshape(bsz, n_chunk, CMP_STRIDE, A_GROUPS, A_HEAD_DIM)
            return u.transpose(0, 3, 1, 2, 4).reshape(bsz, A_GROUPS, n_chunk, CMP_STRIDE * A_HEAD_DIM)

        w1 = jnp.pad(a_cmp_w1[l], ((0, 0), (0, 0), (0, SLOT - A_HEAD_DIM))).astype(BF16)
        w2 = jnp.pad(a_cmp_w2[l], ((0, 0), (0, SLOT - A_HEAD_DIM), (0, SLOT - A_HEAD_DIM))).astype(BF16)
        pos = jnp.broadcast_to(a_cmp_pos[l].reshape(2, 1, CMP_LEN * A_HEAD_DIM),
                               (2, SUBLANES, CMP_LEN * A_HEAD_DIM)).astype(BF16)
        kc, vc = nsa_compress(chunks(PA_KC), chunks(PA_VC), w1, w2, pos, kg)
        oc, nsel = nsa_cmp_attn(qn, kc, vc, ovl_t)
        osel = flash_attention(qr, ks, vs, nsel, hb=1, hpg=A_HPG, real_v=A_HEAD_DIM, mode="select", tq=256, tk=512)
        ow = flash_attention(qr, kw, vw, hb=1, hpg=A_HPG, real_v=A_HEAD_DIM, mode="window", tq=256, tk=256)

        pb3 = pb.reshape(bsz, s, PB_N)
        wuq = _slot_cols(b_w_uq[l], B_HEADS, B_QK).astype(BF16)
        ukv = b_w_ukv[l].reshape(B_KV_LORA, B_HEADS, B_NOPE + B_V)
        wuk = _slot_cols(ukv[:, :, :B_NOPE].reshape(B_KV_LORA, B_HEADS * B_NOPE), B_HEADS, B_NOPE).astype(BF16)
        wuv = _slot_cols(ukv[:, :, B_NOPE:].reshape(B_KV_LORA, B_HEADS * B_V), B_HEADS, B_V).astype(BF16)
        bq, bk, bv = mla_prep(pb3, b_cq_norm_g[l].reshape(1, -1), b_ckv_norm_g[l].reshape(1, -1),
                              wuq, wuk, wuv, tabs_b, _vec_slot(b_q_norm_g[l]), _vec_slot(b_k_norm_g[l]))
        yb = flash_attention(bq, bk, bv, hb=2, hpg=1, real_v=B_V, mode="causal", tq=512, tk=512)

        yc = rglru(pc.reshape(bsz, s, 2 * C_WIDTH), c_conv_w[l], c_conv_b[l].reshape(1, -1),
                   c_w_a[l].astype(BF16), c_b_a[l].reshape(1, -1), c_w_x[l].astype(BF16),
                   c_b_x[l].reshape(1, -1), c_lambda[l].reshape(1, -1))

        x2d = merge(oc.reshape(t, -1), osel.reshape(t, -1), ow.reshape(t, -1), pa,
                    yb.reshape(t, -1), yc.reshape(t, -1), pm, x2d,
                    _slot_rows(w_pa[l], A_HEADS, A_HEAD_DIM).astype(BF16),
                    _slot_rows(w_pb[l], B_HEADS, B_V).astype(BF16),
                    w_pc[l].astype(BF16), w_o[l].astype(BF16))
        x2d = ffn(x2d, ffn_norm_g[l], ffn_w1[l].astype(BF16), ffn_w3[l].astype(BF16), ffn_w2[l].astype(BF16))
    return x2d.reshape(bsz, s, d)
```

```python
import functools

import numpy as np
import jax
import jax.numpy as jnp
from jax import lax
from jax.experimental import pallas as pl
from jax.experimental.pallas import tpu as pltpu

F32 = jnp.float32
BF16 = jnp.bfloat16

D_MODEL = 1024
ROPE_THETA = 10000.0
NORM_EPS = 1e-6
NEG_INF = -1e30
LOG2E = 1.4426950408889634

A_HEADS = 8
A_GROUPS = 2
A_HPG = A_HEADS // A_GROUPS
A_HEAD_DIM = 64
CMP_LEN = 32
CMP_STRIDE = 16
SEL_LEN = 64
SEL_TOPK = 16
WINDOW = 512

B_HEADS = 8
B_Q_LORA = 384
B_KV_LORA = 256
B_NOPE = 64
B_ROPE = 32
B_V = 64
B_QK = B_NOPE + B_ROPE

C_WIDTH = 1024
C_BLOCKS = 8
C_BLOCK_W = C_WIDTH // C_BLOCKS
C_CONV = 4
C_SCALE = 8.0

FFN_HIDDEN = 2816

A_Q = A_HEADS * A_HEAD_DIM
A_KV = A_GROUPS * A_HEAD_DIM
A_GATES = 3 * A_HEADS

LANE = 128
SLOT = LANE
VMEM_LIMIT = 56 * 1024 * 1024

GW = A_GROUPS * SLOT
P_AQ = 0
P_CG = P_AQ + A_HEADS * SLOT
P_CX = P_CG + C_WIDTH
P_MA = P_CX + C_WIDTH
P_MB = P_MA + D_MODEL
P_MC = P_MB + D_MODEL
P_KS = P_MC + D_MODEL
P_VS = P_KS + GW
P_KW = P_VS + GW
P_VW = P_KW + GW
P_CKV = P_VW + GW
P_KC = P_CKV + B_KV_LORA
P_VC = P_KC + A_KV
P_AG = P_VC + A_KV
P_PE = P_AG + SLOT
P_CQ = 21 * B_Q_LORA
P_N = P_CQ + B_Q_LORA
assert P_PE + SLOT <= P_CQ


def _cparams(sem):
    return pltpu.CompilerParams(dimension_semantics=sem, vmem_limit_bytes=VMEM_LIMIT)


def _dot(a, b):
    return jnp.dot(a, b, preferred_element_type=F32)


def _dot_nt(a, b):
    return lax.dot_general(a, b, (((1,), (1,)), ((), ())), preferred_element_type=F32)


def _norm_matmul_kernel(x_ref, g_ref, w_ref, o_ref, h_ref):
    @pl.when(pl.program_id(1) == 0)
    def _():
        x = x_ref[...]
        inv = lax.rsqrt(jnp.mean(x * x, axis=-1, keepdims=True) + NORM_EPS)
        h_ref[...] = ((x * inv) * g_ref[...]).astype(BF16)

    o_ref[...] = _dot(h_ref[...], w_ref[...]).astype(o_ref.dtype)


def norm_matmul(x2d, g, w, tm=1024, tn=1408):
    t, k = x2d.shape
    n = w.shape[1]
    assert n % tn == 0 and tn % LANE == 0
    return pl.pallas_call(
        _norm_matmul_kernel,
        grid=(t // tm, n // tn),
        in_specs=[pl.BlockSpec((tm, k), lambda i, j: (i, 0)),
                  pl.BlockSpec((1, k), lambda i, j: (0, 0)),
                  pl.BlockSpec((k, tn), lambda i, j: (0, j))],
        out_specs=pl.BlockSpec((tm, tn), lambda i, j: (i, j)),
        out_shape=jax.ShapeDtypeStruct((t, n), BF16),
        scratch_shapes=[pltpu.VMEM((tm, k), BF16)],
        compiler_params=_cparams(("parallel", "arbitrary")),
        name="norm_matmul",
    )(x2d, g.reshape(1, k), w)


def _slot_rms(x, g, width):
    x = x.astype(F32)
    ms = jnp.sum(x * x, axis=-1, keepdims=True) * (1.0 / width)
    return (x * lax.rsqrt(ms + NORM_EPS)) * g


def _slot_rope(y, cos, sin_lo, sin_hi, half):
    return y * cos + pltpu.roll(y, LANE - half, 1) * sin_lo + pltpu.roll(y, half, 1) * sin_hi


def _value_t(v, real):
    vt = v.astype(F32).T
    row = lax.broadcasted_iota(jnp.int32, vt.shape, 0)
    return jnp.where(row == real, 1.0, vt).astype(BF16)


def _nsa_prep_kernel(aq_ref, ks_ref, vs_ref, kw_ref, vw_ref, cos_ref, slo_ref, shi_ref, qg_ref, kg_ref,
                     qn_o, qr_o, ks_o, vs_o, kw_o, vw_o, *, tm):
    cos, slo, shi = cos_ref[...], slo_ref[...], shi_ref[...]
    qg, kg = qg_ref[...], kg_ref[...]
    scale = A_HEAD_DIM ** -0.5
    half = A_HEAD_DIM // 2
    for h in range(A_HEADS):
        sl = slice(h * SLOT, (h + 1) * SLOT)
        y = _slot_rms(aq_ref[:, sl], qg, A_HEAD_DIM)
        qn_o[:, sl] = (y * scale).astype(BF16)
        qr_o[:, sl] = (_slot_rope(y, cos, slo, shi, half) * (scale * LOG2E)).astype(BF16)
    t = pl.program_id(1) * tm + lax.broadcasted_iota(jnp.int32, (tm, SLOT), 0)
    lane = lax.broadcasted_iota(jnp.int32, (tm, SLOT), 1)
    blk_bias = jnp.where(lane == A_HEAD_DIM + t // SEL_LEN, NEG_INF, 0.0)
    for g in range(A_GROUPS):
        sl = slice(g * SLOT, (g + 1) * SLOT)
        ksel = _slot_rope(_slot_rms(ks_ref[:, sl], kg, A_HEAD_DIM), cos, slo, shi, half)
        ks_o[:, sl] = (ksel + blk_bias).astype(BF16)
        kw_o[:, sl] = _slot_rope(_slot_rms(kw_ref[:, sl], kg, A_HEAD_DIM), cos, slo, shi, half).astype(BF16)
        vs_o[sl, :] = _value_t(vs_ref[:, sl], A_HEAD_DIM)
        vw_o[sl, :] = _value_t(vw_ref[:, sl], A_HEAD_DIM)


def nsa_prep(pa, tabs, qg, kg, tm=512):
    bsz, s, _ = pa.shape
    gw = A_GROUPS * SLOT
    assert s // SEL_LEN <= SLOT - A_HEAD_DIM, "selection blocks must fit the free lanes of a head slot"

    def col(width, off):
        return pl.BlockSpec((None, tm, width), lambda b, i: (b, i, off // width))

    tab = pl.BlockSpec((None, tm, LANE), lambda b, i: (b, i, 0))
    vec = pl.BlockSpec((1, LANE), lambda b, i: (0, 0))
    out_q = pl.BlockSpec((None, tm, A_HEADS * SLOT), lambda b, i: (b, i, 0))
    out_k = pl.BlockSpec((None, tm, gw), lambda b, i: (b, i, 0))
    out_vt = pl.BlockSpec((None, gw, tm), lambda b, i: (b, 0, i))
    sq = jax.ShapeDtypeStruct((bsz, s, A_HEADS * SLOT), BF16)
    sk = jax.ShapeDtypeStruct((bsz, s, gw), BF16)
    svt = jax.ShapeDtypeStruct((bsz, gw, s), BF16)
    return pl.pallas_call(
        functools.partial(_nsa_prep_kernel, tm=tm),
        grid=(bsz, s // tm),
        in_specs=[col(A_HEADS * SLOT, P_AQ), col(gw, P_KS), col(gw, P_VS), col(gw, P_KW), col(gw, P_VW),
                  tab, tab, tab, vec, vec],
        out_specs=[out_q, out_q, out_k, out_vt, out_k, out_vt],
        out_shape=[sq, sq, sk, svt, sk, svt],
        compiler_params=_cparams(("parallel", "parallel")),
        name="nsa_prep",
    )(pa, pa, pa, pa, pa, *tabs, qg, kg)


def _compress_kernel(uk_ref, uv_ref, w1_ref, w2_ref, pos_ref, kg_ref, kc_o, vc_o):
    half = CMP_STRIDE * A_HEAD_DIM
    n = uk_ref.shape[0]
    for z, (u_ref, o_ref) in enumerate(((uk_ref, kc_o), (uv_ref, vc_o))):
        u = u_ref[...].astype(BF16)
        first = _dot(u, w1_ref[z, :half, :])
        second = _dot(u, w1_ref[z, half:, :])
        posc = _dot(pos_ref[z], w1_ref[z])[0:1, :]
        hid = first + pltpu.roll(second, n - 1, 0) + posc
        hid = hid * jax.nn.sigmoid(hid)
        comp = _dot(hid.astype(BF16), w2_ref[z])
        if z == 0:
            comp = _slot_rms(comp, kg_ref[...], A_HEAD_DIM)
        o_ref[...] = comp.astype(BF16)


def nsa_compress(uk, uv, w1, w2, pos, kg):
    bsz, g, n, f = uk.shape
    u_spec = pl.BlockSpec((None, None, n, f), lambda b, gi: (b, gi, 0, 0))
    o_spec = pl.BlockSpec((None, None, n, SLOT), lambda b, gi: (b, gi, 0, 0))
    so = jax.ShapeDtypeStruct((bsz, g, n, SLOT), BF16)
    return pl.pallas_call(
        _compress_kernel,
        grid=(bsz, g),
        in_specs=[u_spec, u_spec,
                  pl.BlockSpec(w1.shape, lambda b, gi: (0, 0, 0)),
                  pl.BlockSpec(w2.shape, lambda b, gi: (0, 0, 0)),
                  pl.BlockSpec(pos.shape, lambda b, gi: (0, 0, 0)),
                  pl.BlockSpec((1, LANE), lambda b, gi: (0, 0))],
        out_specs=[o_spec, o_spec],
        out_shape=[so, so],
        compiler_params=_cparams(("parallel", "parallel")),
        name="nsa_compress",
    )(uk, uv, w1, w2, pos, kg)


def _cmp_attn_kernel(q_ref, kc_ref, vc_ref, ovl_ref, oc_o, sel_o, *, tq):
    t0 = pl.program_id(2) * tq
    kc = kc_ref[...]
    vc = vc_ref[...]
    n_cmp = kc.shape[0]
    n_sel = ovl_ref.shape[0]
    t = t0 + lax.broadcasted_iota(jnp.int32, (tq, n_cmp), 0)
    n = lax.broadcasted_iota(jnp.int32, (tq, n_cmp), 1)
    mask = t >= n * CMP_STRIDE + (CMP_LEN - 1)
    psum = jnp.zeros((tq, n_cmp), F32)
    for h in range(A_HPG):
        sl = slice(h * SLOT, (h + 1) * SLOT)
        s = jnp.where(mask, _dot_nt(q_ref[:, sl], kc), NEG_INF)
        m = jnp.max(s, axis=-1, keepdims=True)
        p = jnp.where(mask, jnp.exp(s - m), 0.0)
        l = jnp.sum(p, axis=-1, keepdims=True)
        p = p / jnp.where(l > 0.0, l, 1.0)
        psum = psum + p
        oc_o[:, sl] = _dot(p.astype(BF16), vc).astype(oc_o.dtype)
    imp = lax.dot_general(ovl_ref[...], psum, (((1,), (1,)), ((), ())),
                          precision=lax.Precision.HIGHEST, preferred_element_type=F32)
    j = lax.broadcasted_iota(jnp.int32, (n_sel, tq), 0)
    cur = (t0 + lax.broadcasted_iota(jnp.int32, (n_sel, tq), 1)) // SEL_LEN
    big = -NEG_INF
    imp = jnp.where(j == 0, 3.0 * big, jnp.where(j == cur, 2.0 * big, jnp.where(j == cur - 1, big, imp)))
    imp = jnp.where(j > cur, NEG_INF, imp)
    work = imp
    kth = None
    for _ in range(min(SEL_TOPK, n_sel)):
        kth = jnp.max(work, axis=0, keepdims=True)
        work = jnp.where(work >= kth, -jnp.inf, work)
    notsel = jnp.where(jnp.logical_and(imp >= kth, j <= cur), 0.0, 1.0)
    free = SLOT - A_HEAD_DIM
    parts = [jnp.zeros((A_HEAD_DIM, tq), F32), notsel]
    if n_sel < free:
        parts.append(jnp.zeros((free - n_sel, tq), F32))
    sel_o[...] = jnp.concatenate(parts, axis=0).T.astype(sel_o.dtype)


def nsa_cmp_attn(qn, kc, vc, ovl_t, tq=256):
    bsz, s, _ = qn.shape
    n_cmp = kc.shape[2]
    n_sel = ovl_t.shape[0]
    gq = A_HPG * SLOT
    kv_spec = pl.BlockSpec((None, None, n_cmp, SLOT), lambda b, g, i: (b, g, 0, 0))
    return pl.pallas_call(
        functools.partial(_cmp_attn_kernel, tq=tq),
        grid=(bsz, A_GROUPS, s // tq),
        in_specs=[pl.BlockSpec((None, tq, gq), lambda b, g, i: (b, i, g)),
                  kv_spec, kv_spec,
                  pl.BlockSpec(ovl_t.shape, lambda b, g, i: (0, 0))],
        out_specs=[pl.BlockSpec((None, tq, gq), lambda b, g, i: (b, i, g)),
                   pl.BlockSpec((None, None, tq, SLOT), lambda b, g, i: (b, g, i, 0))],
        out_shape=[jax.ShapeDtypeStruct((bsz, s, A_HEADS * SLOT), BF16),
                   jax.ShapeDtypeStruct((bsz, A_GROUPS, s, SLOT), BF16)],
        compiler_params=_cparams(("parallel", "parallel", "parallel")),
        name="nsa_cmp_attn",
    )(qn, kc, vc, ovl_t)


def _softmax_tile(s, m_prev, acc_prev, vt):
    m_new = jnp.maximum(m_prev, jnp.max(s, axis=0, keepdims=True))
    alpha = jnp.exp2(m_prev - m_new)
    p = jnp.exp2(s - m_new).astype(BF16)
    return m_new, alpha * acc_prev + _dot(vt, p)


def _visible(kind, k0, q0, tk, tq, n):
    key = k0 + lax.broadcasted_iota(jnp.int32, (tk, n), 0)
    qry = q0 + (lax.broadcasted_iota(jnp.int32, (tk, n), 1) & (tq - 1))
    return key <= qry if kind == "causal" else key > qry - WINDOW


def _store_heads(o_ref, b, hpg, tq, acc, real_v):
    out = acc / acc[real_v:real_v + 1, :]
    for h in range(hpg):
        o_ref[:, (b * hpg + h) * SLOT:(b * hpg + h + 1) * SLOT] = out[:, h * tq:(h + 1) * tq].T.astype(o_ref.dtype)


def _flash_kernel(qi_ref, kj_ref, fl_ref, *refs, hb, hpg, real_v, select, tq, tk):
    if select:
        q_ref, k_ref, vt_ref, nsel_ref, o_ref, qst_s, m_s, acc_s = refs
    else:
        q_ref, k_ref, vt_ref, o_ref, qst_s, m_s, acc_s = refs
    step_id = pl.program_id(2)
    q0 = qi_ref[step_id] * tq
    k0 = kj_ref[step_id] * tk
    flags = fl_ref[step_id]
    n = hpg * tq

    @pl.when((flags & STEP_FIRST) != 0)
    def _():
        for b in range(hb):
            for h in range(hpg):
                qh = q_ref[:, (b * hpg + h) * SLOT:(b * hpg + h + 1) * SLOT]
                if select:
                    qh = qh + nsel_ref[b]
                qst_s[b, h * tq:(h + 1) * tq, :] = qh
        m_s[...] = jnp.full(m_s.shape, NEG_INF, F32)
        acc_s[...] = jnp.zeros(acc_s.shape, F32)

    def step(kind):
        visible = _visible(kind, k0, q0, tk, tq, n) if kind != "none" else None
        for b in range(hb):
            s = _dot_nt(k_ref[:, b * SLOT:(b + 1) * SLOT], qst_s[b])
            if visible is not None:
                s = jnp.where(visible, s, NEG_INF)
            m_s[b], acc_s[b] = _softmax_tile(s, m_s[b], acc_s[b], vt_ref[b * SLOT:(b + 1) * SLOT, :])

    @pl.when((flags & STEP_DIAG) == 0)
    def _():
        step("none")

    @pl.when((flags & STEP_DIAG) != 0)
    def _():
        step("causal")

    @pl.when((flags & STEP_LAST) != 0)
    def _():
        for b in range(hb):
            _store_heads(o_ref, b, hpg, tq, acc_s[b], real_v)


STEP_FIRST, STEP_LAST, STEP_DIAG = 1, 2, 4


def _flash_steps(n_q, tq, tk):
    qi, kj, fl = [], [], []
    for i in range(n_q):
        last = (i * tq + tq - 1) // tk
        for j in range(last + 1):
            diag = j * tk + tk - 1 > i * tq
            qi.append(i)
            kj.append(j)
            fl.append((STEP_FIRST if j == 0 else 0) | (STEP_LAST if j == last else 0) | (STEP_DIAG if diag else 0))
    return [jnp.asarray(np.asarray(a, np.int32)) for a in (qi, kj, fl)]


def flash_attention(q, k, vt, nsel=None, *, hb, hpg, real_v, tq, tk):
    bsz, s, qw = q.shape
    groups = qw // (hpg * SLOT)
    assert tq & (tq - 1) == 0 and groups % hb == 0
    tables = _flash_steps(s // tq, tq, tk)
    n_steps = tables[0].shape[0]
    in_specs = [pl.BlockSpec((None, tq, hb * hpg * SLOT), lambda b, g, t, qi, kj, fl: (b, qi[t], g)),
                pl.BlockSpec((None, tk, hb * SLOT), lambda b, g, t, qi, kj, fl: (b, kj[t], g)),
                pl.BlockSpec((None, hb * SLOT, tk), lambda b, g, t, qi, kj, fl: (b, g, kj[t]))]
    args = [q, k, vt]
    if nsel is not None:
        in_specs.append(pl.BlockSpec((None, hb, tq, SLOT), lambda b, g, t, qi, kj, fl: (b, g, qi[t], 0)))
        args.append(nsel)
    return pl.pallas_call(
        functools.partial(_flash_kernel, hb=hb, hpg=hpg, real_v=real_v, select=nsel is not None, tq=tq, tk=tk),
        grid_spec=pltpu.PrefetchScalarGridSpec(
            num_scalar_prefetch=3,
            grid=(bsz, groups // hb, n_steps),
            in_specs=in_specs,
            out_specs=pl.BlockSpec((None, tq, hb * hpg * SLOT), lambda b, g, t, qi, kj, fl: (b, qi[t], g)),
            scratch_shapes=[pltpu.VMEM((hb, hpg * tq, SLOT), BF16), pltpu.VMEM((hb, 1, hpg * tq), F32),
                            pltpu.VMEM((hb, SLOT, hpg * tq), F32)]),
        out_shape=jax.ShapeDtypeStruct((bsz, s, qw), BF16),
        compiler_params=_cparams(("parallel", "parallel", "arbitrary")),
        name="flash_select" if nsel is not None else "flash_causal",
    )(*tables, *args)


def _window_kernel(q_ref, *refs, hb, hpg, real_v, tq, nk):
    k_refs, vt_refs, o_ref = refs[:nk], refs[nk:2 * nk], refs[2 * nk]
    i = pl.program_id(2)
    q0 = i * tq
    n = hpg * tq
    for b in range(hb):
        qst = jnp.concatenate([q_ref[:, (b * hpg + h) * SLOT:(b * hpg + h + 1) * SLOT] for h in range(hpg)], axis=0)
        m = jnp.full((1, n), NEG_INF, F32)
        acc = jnp.zeros((SLOT, n), F32)
        for t in range(nk):
            back = nk - 1 - t
            s = _dot_nt(k_refs[t][:, b * SLOT:(b + 1) * SLOT], qst)
            if t == 0:
                s = jnp.where(jnp.logical_and(_visible("lower", q0 - back * tq, q0, tq, tq, n), i >= back), s, NEG_INF)
            elif t == nk - 1:
                s = jnp.where(_visible("causal", q0, q0, tq, tq, n), s, NEG_INF)
            else:
                s = jnp.where(i >= back, s, NEG_INF)
            m, acc = _softmax_tile(s, m, acc, vt_refs[t][b * SLOT:(b + 1) * SLOT, :])
        _store_heads(o_ref, b, hpg, tq, acc, real_v)


def window_attention(q, k, vt, *, hb, hpg, real_v, tq):
    bsz, s, qw = q.shape
    groups = qw // (hpg * SLOT)
    assert tq & (tq - 1) == 0 and groups % hb == 0 and WINDOW % tq == 0
    nk = WINDOW // tq + 1

    def k_spec(t):
        return pl.BlockSpec((None, tq, hb * SLOT), lambda b, g, i: (b, jnp.maximum(i - (nk - 1 - t), 0), g))

    def vt_spec(t):
        return pl.BlockSpec((None, hb * SLOT, tq), lambda b, g, i: (b, g, jnp.maximum(i - (nk - 1 - t), 0)))

    q_spec = pl.BlockSpec((None, tq, hb * hpg * SLOT), lambda b, g, i: (b, i, g))
    return pl.pallas_call(
        functools.partial(_window_kernel, hb=hb, hpg=hpg, real_v=real_v, tq=tq, nk=nk),
        grid=(bsz, groups // hb, s // tq),
        in_specs=[q_spec] + [k_spec(t) for t in range(nk)] + [vt_spec(t) for t in range(nk)],
        out_specs=q_spec,
        out_shape=jax.ShapeDtypeStruct((bsz, s, qw), BF16),
        compiler_params=_cparams(("parallel", "parallel", "parallel")),
        name="flash_window",
    )(q, *([k] * nk), *([vt] * nk))


def _mla_prep_kernel(ckv_ref, kpe_ref, cq_ref, cqg_ref, ckvg_ref, wuq_ref, wuk_ref, wuv_ref,
                     cos_ref, slo_ref, shi_ref, qg_ref, kg_ref, q_o, k_o, v_o):
    def rms(x, g):
        x = x.astype(F32)
        return ((x * lax.rsqrt(jnp.mean(x * x, axis=-1, keepdims=True) + NORM_EPS)) * g).astype(BF16)

    cq = rms(cq_ref[...], cqg_ref[...])
    ckv = rms(ckv_ref[...], ckvg_ref[...])
    q = _dot(cq, wuq_ref[...])
    kn = _dot(ckv, wuk_ref[...])
    vv = _dot(ckv, wuv_ref[...])
    kpe = kpe_ref[...].astype(F32)
    cos, slo, shi = cos_ref[...], slo_ref[...], shi_ref[...]
    qg, kg = qg_ref[...], kg_ref[...]
    scale = B_QK ** -0.5
    half = B_ROPE // 2
    for h in range(B_HEADS):
        sl = slice(h * SLOT, (h + 1) * SLOT)
        yq = _slot_rope(_slot_rms(q[:, sl], qg, B_QK), cos, slo, shi, half)
        q_o[:, sl] = (yq * (scale * LOG2E)).astype(BF16)
        yk = _slot_rope(_slot_rms(kn[:, sl] + kpe, kg, B_QK), cos, slo, shi, half)
        k_o[:, sl] = yk.astype(BF16)
        v_o[sl, :] = _value_t(vv[:, sl], B_V)


def mla_prep(pb, cqg, ckvg, wuq, wuk, wuv, tabs, qg, kg, tm=256):
    bsz, s, _ = pb.shape
    hw = B_HEADS * SLOT

    def const(shape):
        return pl.BlockSpec(shape, lambda b, i: (0,) * len(shape))

    tab = pl.BlockSpec((None, tm, LANE), lambda b, i: (b, i, 0))
    out = pl.BlockSpec((None, tm, hw), lambda b, i: (b, i, 0))
    so = jax.ShapeDtypeStruct((bsz, s, hw), BF16)
    return pl.pallas_call(
        _mla_prep_kernel,
        grid=(bsz, s // tm),
        in_specs=[pl.BlockSpec((None, tm, B_KV_LORA), lambda b, i: (b, i, P_CKV // B_KV_LORA)),
                  pl.BlockSpec((None, tm, SLOT), lambda b, i: (b, i, P_PE // SLOT)),
                  pl.BlockSpec((None, tm, B_Q_LORA), lambda b, i: (b, i, P_CQ // B_Q_LORA)),
                  const((1, B_Q_LORA)), const((1, B_KV_LORA)),
                  const(wuq.shape), const(wuk.shape), const(wuv.shape),
                  tab, tab, tab, const((1, LANE)), const((1, LANE))],
        out_specs=[out, out, pl.BlockSpec((None, hw, tm), lambda b, i: (b, 0, i))],
        out_shape=[so, so, jax.ShapeDtypeStruct((bsz, hw, s), BF16)],
        compiler_params=_cparams(("parallel", "parallel")),
        name="mla_prep",
    )(pb, pb, pb, cqg, ckvg, wuq, wuk, wuv, *tabs, qg, kg)


SUBLANES = 8


def _rglru_kernel(cg_ref, cx_ref, cw_ref, cb_ref, wa_ref, ba_ref, wx_ref, bx_ref, lam_ref, o_ref,
                  xbuf, a_s, b_s, carry, *, ts):
    @pl.when(pl.program_id(1) == 0)
    def _():
        xbuf[0:SUBLANES, :] = jnp.zeros((SUBLANES, C_WIDTH), F32)
        carry[...] = jnp.zeros(carry.shape, F32)

    xr = cx_ref[...].astype(F32)
    xbuf[SUBLANES:SUBLANES + ts, :] = xr
    xc = cb_ref[...] + xr * cw_ref[C_CONV - 1:C_CONV, :]
    for w in range(C_CONV - 1):
        back = C_CONV - 1 - w
        xc = xc + xbuf[pl.ds(SUBLANES - back, ts), :] * cw_ref[w:w + 1, :]
    xbuf[0:SUBLANES, :] = xr[ts - SUBLANES:ts, :]

    for n in range(C_BLOCKS):
        sl = slice(n * C_BLOCK_W, (n + 1) * C_BLOCK_W)
        xb = xc[:, sl].astype(BF16)
        a_s[:, sl] = _dot(xb, wa_ref[n])
        b_s[:, sl] = _dot(xb, wx_ref[n])
    r = jax.nn.sigmoid(a_s[...] + ba_ref[...])
    ig = jax.nn.sigmoid(b_s[...] + bx_ref[...])
    nlam = -lam_ref[...]
    softplus = jnp.maximum(nlam, 0.0) + jnp.log1p(jnp.exp(-jnp.abs(nlam)))
    log_a = (-C_SCALE) * r * softplus
    a_s[...] = jnp.exp(log_a)
    th = jnp.tanh(log_a)
    b_s[...] = jnp.sqrt((-2.0 * th) / (1.0 - th)) * (ig * xc)

    row = lax.broadcasted_iota(jnp.int32, (SUBLANES, C_WIDTH), 0)

    def group(gi, h_prev):
        off = pl.multiple_of(gi * SUBLANES, SUBLANES)
        a = a_s[pl.ds(off, SUBLANES), :]
        b = b_s[pl.ds(off, SUBLANES), :]
        for d in (1, 2, 4):
            a_sh = pltpu.roll(a, d, 0)
            b_sh = pltpu.roll(b, d, 0)
            keep = row >= d
            b = jnp.where(keep, a * b_sh + b, b)
            a = jnp.where(keep, a * a_sh, a)
        h = b + a * h_prev
        b_s[pl.ds(off, SUBLANES), :] = h
        return h[SUBLANES - 1:SUBLANES, :]

    carry[...] = lax.fori_loop(0, ts // SUBLANES, group, carry[...])
    gate = cg_ref[...].astype(F32)
    gelu =0.5 * gate * (1.0 + jnp.tanh(0.7978845608028654 * (gate + 0.044715 * (gate * gate * gate))))
    o_ref[...] = (gelu * b_s[...]).astype(o_ref.dtype)


def rglru(pc, cw, cb, wa, ba, wx, bx, lam, ts=256):
    bsz, s, _ = pc.shape

    def const(shape):
        return pl.BlockSpec(shape, lambda b, i: (0,) * len(shape))

    vec = const((1, C_WIDTH))
    return pl.pallas_call(
        functools.partial(_rglru_kernel, ts=ts),
        grid=(bsz, s // ts),
        in_specs=[pl.BlockSpec((None, ts, C_WIDTH), lambda b, i: (b, i, P_CG // C_WIDTH)),
                  pl.BlockSpec((None, ts, C_WIDTH), lambda b, i: (b, i, P_CX // C_WIDTH)),
                  const((C_CONV, C_WIDTH)), vec, const(wa.shape), vec, const(wx.shape), vec, vec],
        out_specs=pl.BlockSpec((None, ts, C_WIDTH), lambda b, i: (b, i, 0)),
        out_shape=jax.ShapeDtypeStruct((bsz, s, C_WIDTH), BF16),
        scratch_shapes=[pltpu.VMEM((ts + SUBLANES, C_WIDTH), F32), pltpu.VMEM((ts, C_WIDTH), F32),
                        pltpu.VMEM((ts, C_WIDTH), F32), pltpu.VMEM((1, C_WIDTH), F32)],
        compiler_params=_cparams(("parallel", "arbitrary")),
        name="rglru",
    )(pc, pc, cw, cb, wa, ba, wx, bx, lam)


def _merge_kernel(oc_ref, os_ref, ow_ref, ag_ref, yb_ref, yc_ref, ma_ref, mb_ref, mc_ref, x_ref,
                  wpa_ref, wpb_ref, wpc_ref, wo_ref, o_ref, ya_s):
    gates = jax.nn.sigmoid(ag_ref[...].astype(F32))
    for h in range(A_HEADS):
        sl = slice(h * SLOT, (h + 1) * SLOT)
        y = (gates[:, 3 * h:3 * h + 1] * oc_ref[:, sl].astype(F32)
             + gates[:, 3 * h + 1:3 * h + 2] * os_ref[:, sl].astype(F32)
             + gates[:, 3 * h + 2:3 * h + 3] * ow_ref[:, sl].astype(F32))
        ya_s[:, sl] = y.astype(BF16)
    merged = (jax.nn.sigmoid(ma_ref[...].astype(F32)) * _dot(ya_s[...], wpa_ref[...])
              + jax.nn.sigmoid(mb_ref[...].astype(F32)) * _dot(yb_ref[...], wpb_ref[...])
              + jax.nn.sigmoid(mc_ref[...].astype(F32)) * _dot(yc_ref[...], wpc_ref[...]))
    o_ref[...] = x_ref[...] + _dot(merged.astype(BF16), wo_ref[...])


def merge(oc, osel, ow, pa2d, yb, yc, pm, x2d, wpa, wpb, wpc, wo, tm=256):
    t = x2d.shape[0]
    d = D_MODEL

    def rows(width, off=0):
        return pl.BlockSpec((tm, width), lambda i: (i, off // width))

    w_spec = pl.BlockSpec((d, d), lambda i: (0, 0))
    return pl.pallas_call(
        _merge_kernel,
        grid=(t // tm,),
        in_specs=[rows(d), rows(d), rows(d), rows(SLOT, P_AG), rows(d), rows(d),
                  rows(d, P_MA), rows(d, P_MB), rows(d, P_MC), rows(d),
                  w_spec, w_spec, w_spec, w_spec],
        out_specs=rows(d),
        out_shape=jax.ShapeDtypeStruct((t, d), F32),
        scratch_shapes=[pltpu.VMEM((tm, d), BF16)],
        compiler_params=_cparams(("parallel",)),
        name="merge",
    )(oc, osel, ow, pa2d, yb, yc, pm, pm, pm, x2d, wpa, wpb, wpc, wo)


def _ffn_kernel(x_ref, g_ref, w1_ref, w3_ref, w2_ref, o_ref, h_s, acc_s, *, nf):
    f = pl.program_id(1)

    @pl.when(f == 0)
    def _():
        x = x_ref[...]
        inv = lax.rsqrt(jnp.mean(x * x, axis=-1, keepdims=True) + NORM_EPS)
        h_s[...] = ((x * inv) * g_ref[...]).astype(BF16)
        acc_s[...] = jnp.zeros(acc_s.shape, F32)

    h = h_s[...]
    u = _dot(h, w1_ref[...])
    z = (u * jax.nn.sigmoid(u)) * _dot(h, w3_ref[...])
    acc_s[...] += _dot(z.astype(BF16), w2_ref[...])

    @pl.when(f == nf - 1)
    def _():
        o_ref[...] = x_ref[...] + acc_s[...]


def ffn(x2d, g, w1, w3, w2, tm=1024, tf=256):
    t, d = x2d.shape
    hidden = w1.shape[1]
    nf = hidden // tf
    return pl.pallas_call(
        functools.partial(_ffn_kernel, nf=nf),
        grid=(t // tm, nf),
        in_specs=[pl.BlockSpec((tm, d), lambda i, f: (i, 0)),
                  pl.BlockSpec((1, d), lambda i, f: (0, 0)),
                  pl.BlockSpec((d, tf), lambda i, f: (0, f)),
                  pl.BlockSpec((d, tf), lambda i, f: (0, f)),
                  pl.BlockSpec((tf, d), lambda i, f: (f, 0))],
        out_specs=pl.BlockSpec((tm, d), lambda i, f: (i, 0)),
        out_shape=jax.ShapeDtypeStruct((t, d), F32),
        scratch_shapes=[pltpu.VMEM((tm, d), BF16), pltpu.VMEM((tm, d), F32)],
        compiler_params=_cparams(("parallel", "arbitrary")),
        name="ffn",
    )(x2d, g.reshape(1, d), w1, w3, w2)


def _pad_cols(w, width):
    return jnp.pad(w, ((0, 0), (0, width - w.shape[1])))


def _slot_cols(w, heads, real):
    k = w.shape[0]
    return jnp.pad(w.reshape(k, heads, real), ((0, 0), (0, 0), (0, SLOT - real))).reshape(k, heads * SLOT)


def _slot_rows(w, heads, real):
    n = w.shape[1]
    return jnp.pad(w.reshape(heads, real, n), ((0, 0), (0, SLOT - real), (0, 0))).reshape(heads * SLOT, n)


def _split_w_in(w):
    sizes = (A_Q, A_KV, A_KV, A_KV, A_KV, A_KV, A_KV, A_GATES, B_Q_LORA, B_KV_LORA, B_ROPE,
             C_WIDTH, C_WIDTH, D_MODEL, D_MODEL, D_MODEL)
    offs = np.cumsum((0,) + sizes)
    (aq, akc, avc, aks, avs, akw, avw, ag, bcq, bckv, bkpe, cg, cx, ma, mb, mc) = [
        w[:, offs[n]:offs[n + 1]] for n in range(len(sizes))]
    kpe_slot = jnp.pad(bkpe, ((0, 0), (B_NOPE, SLOT - B_NOPE - B_ROPE)))
    pieces = [(P_AQ, _slot_cols(aq, A_HEADS, A_HEAD_DIM)), (P_CG, cg), (P_CX, cx), (P_MA, ma), (P_MB, mb),
              (P_MC, mc), (P_KS, _slot_cols(aks, A_GROUPS, A_HEAD_DIM)), (P_VS, _slot_cols(avs, A_GROUPS, A_HEAD_DIM)),
              (P_KW, _slot_cols(akw, A_GROUPS, A_HEAD_DIM)), (P_VW, _slot_cols(avw, A_GROUPS, A_HEAD_DIM)),
              (P_CKV, bckv), (P_KC, akc), (P_VC, avc), (P_AG, _pad_cols(ag, SLOT)), (P_PE, kpe_slot), (P_CQ, bcq)]
    cols, at = [], 0
    for off, piece in pieces:
        assert off >= at
        if off > at:
            cols.append(jnp.zeros((w.shape[0], off - at), w.dtype))
        cols.append(piece)
        at = off + piece.shape[1]
    assert at == P_N
    return jnp.concatenate(cols, axis=1).astype(BF16)


def _rope_tables(positions, half, lo):
    inv = ROPE_THETA ** (-jnp.arange(half, dtype=F32) / half)
    ang = positions.astype(F32)[..., None] * inv
    cos, sin = jnp.cos(ang), jnp.sin(ang)
    shape = ang.shape[:-1]
    ones = jnp.ones(shape + (lo,), F32)
    tail = SLOT - lo - 2 * half
    z = lambda n: jnp.zeros(shape + (n,), F32)
    cos_t = jnp.concatenate([ones, cos, cos, jnp.ones(shape + (tail,), F32)], axis=-1)
    sin_lo = jnp.concatenate([z(lo), -sin, z(half), z(tail)], axis=-1)
    sin_hi = jnp.concatenate([z(lo), z(half), sin, z(tail)], axis=-1)
    return cos_t, sin_lo, sin_hi


def _overlap_t(s):
    n_cmp = s // CMP_STRIDE
    n_sel = s // SEL_LEN
    starts = np.arange(n_cmp) * CMP_STRIDE
    sel_starts = np.arange(n_sel) * SEL_LEN
    ovl = np.clip(np.minimum(starts[:, None] + CMP_LEN, sel_starts[None, :] + SEL_LEN)
                  - np.maximum(starts[:, None], sel_starts[None, :]), 0, None) / CMP_LEN
    ovl[(s - CMP_LEN) // CMP_STRIDE + 1:, :] = 0.0
    return jnp.asarray(ovl.T, dtype=F32)


def _vec_slot(g):
    return jnp.pad(g, (0, SLOT - g.shape[0])).reshape(1, SLOT).astype(F32)


def kernel(x, positions, mix_norm_g, w_in, a_q_norm_g, a_k_norm_g, a_cmp_pos, a_cmp_w1, a_cmp_w2, b_cq_norm_g, b_ckv_norm_g, b_w_uq, b_w_ukv, b_q_norm_g, b_k_norm_g, c_conv_w, c_conv_b, c_w_a, c_b_a, c_w_x, c_b_x, c_lambda, w_pa, w_pb, w_pc, w_o, ffn_norm_g, ffn_w1, ffn_w3, ffn_w2):
    bsz, s, d = x.shape
    t = bsz * s
    depth = w_in.shape[0]
    n_chunk = s // CMP_STRIDE
    tabs_a = _rope_tables(positions, A_HEAD_DIM // 2, 0)
    tabs_b = _rope_tables(positions, B_ROPE // 2, B_NOPE)
    ovl_t = _overlap_t(s)

    x2d = x.reshape(t, d)
    for l in range(depth):
        pj = norm_matmul(x2d, mix_norm_g[l], _split_w_in(w_in[l]))
        pj3 = pj.reshape(bsz, s, P_N)

        qg, kg = _vec_slot(a_q_norm_g[l]), _vec_slot(a_k_norm_g[l])
        qn, qr, ks, vs, kw, vw = nsa_prep(pj3, tabs_a, qg, kg)

        def chunks(off):
            u = pj3[:, :, off:off + A_KV].reshape(bsz, n_chunk, CMP_STRIDE, A_GROUPS, A_HEAD_DIM)
            return u.transpose(0, 3, 1, 2, 4).reshape(bsz, A_GROUPS, n_chunk, CMP_STRIDE * A_HEAD_DIM)

        w1 = jnp.pad(a_cmp_w1[l], ((0, 0), (0, 0), (0, SLOT - A_HEAD_DIM))).astype(BF16)
        w2 = jnp.pad(a_cmp_w2[l], ((0, 0), (0, SLOT - A_HEAD_DIM), (0, SLOT - A_HEAD_DIM))).astype(BF16)
        pos = jnp.broadcast_to(a_cmp_pos[l].reshape(2, 1, CMP_LEN * A_HEAD_DIM),
                               (2, SUBLANES, CMP_LEN * A_HEAD_DIM)).astype(BF16)
        kc, vc = nsa_compress(chunks(P_KC), chunks(P_VC), w1, w2, pos, kg)
        oc, nsel = nsa_cmp_attn(qn, kc, vc, ovl_t)
        osel = flash_attention(qr, ks, vs, nsel, hb=A_GROUPS, hpg=A_HPG, real_v=A_HEAD_DIM, tq=512, tk=512)
        ow = window_attention(qr, kw, vw, hb=A_GROUPS, hpg=A_HPG, real_v=A_HEAD_DIM, tq=256)

        wuq = _slot_cols(b_w_uq[l], B_HEADS, B_QK).astype(BF16)
        ukv = b_w_ukv[l].reshape(B_KV_LORA, B_HEADS, B_NOPE + B_V)
        wuk = _slot_cols(ukv[:, :, :B_NOPE].reshape(B_KV_LORA, B_HEADS * B_NOPE), B_HEADS, B_NOPE).astype(BF16)
        wuv = _slot_cols(ukv[:, :, B_NOPE:].reshape(B_KV_LORA, B_HEADS * B_V), B_HEADS, B_V).astype(BF16)
        bq, bk, bv = mla_prep(pj3, b_cq_norm_g[l].reshape(1, -1), b_ckv_norm_g[l].reshape(1, -1),
                              wuq, wuk, wuv, tabs_b, _vec_slot(b_q_norm_g[l]), _vec_slot(b_k_norm_g[l]))
        yb = flash_attention(bq, bk, bv, hb=4, hpg=1, real_v=B_V, tq=512, tk=512)

        yc = rglru(pj3, c_conv_w[l], c_conv_b[l].reshape(1, -1),
                   c_w_a[l].astype(BF16), c_b_a[l].reshape(1, -1), c_w_x[l].astype(BF16),
                   c_b_x[l].reshape(1, -1), c_lambda[l].reshape(1, -1))

        x2d = merge(oc.reshape(t, -1), osel.reshape(t, -1), ow.reshape(t, -1), pj,
                    yb.reshape(t, -1), yc.reshape(t, -1), pj, x2d,
                    _slot_rows(w_pa[l], A_HEADS, A_HEAD_DIM).astype(BF16),
                    _slot_rows(w_pb[l], B_HEADS, B_V).astype(BF16),
                    w_pc[l].astype(BF16), w_o[l].astype(BF16))
        x2d = ffn(x2d, ffn_norm_g[l], ffn_w1[l].astype(BF16), ffn_w3[l].astype(BF16), ffn_w2[l].astype(BF16))
    return x2d.reshape(bsz, s, d)
```

```python
import functools

import numpy as np
import jax
import jax.numpy as jnp
from jax import lax
from jax.experimental import pallas as pl
from jax.experimental.pallas import tpu as pltpu

F32 = jnp.float32
BF16 = jnp.bfloat16

D_MODEL = 1024
ROPE_THETA = 10000.0
NORM_EPS = 1e-6
NEG_INF = -1e30
LOG2E = 1.4426950408889634

A_HEADS = 8
A_GROUPS = 2
A_HPG = A_HEADS // A_GROUPS
A_HEAD_DIM = 64
CMP_LEN = 32
CMP_STRIDE = 16
SEL_LEN = 64
SEL_TOPK = 16
WINDOW = 512

B_HEADS = 8
B_Q_LORA = 384
B_KV_LORA = 256
B_NOPE = 64
B_ROPE = 32
B_V = 64
B_QK = B_NOPE + B_ROPE

C_WIDTH = 1024
C_BLOCKS = 8
C_BLOCK_W = C_WIDTH // C_BLOCKS
C_CONV = 4
C_SCALE = 8.0

FFN_HIDDEN = 2816

A_Q = A_HEADS * A_HEAD_DIM
A_KV = A_GROUPS * A_HEAD_DIM
A_GATES = 3 * A_HEADS

LANE = 128
SLOT = LANE
VMEM_LIMIT = 56 * 1024 * 1024

GW = A_GROUPS * SLOT
P_AQ = 0
P_CG = P_AQ + A_HEADS * SLOT
P_CX = P_CG + C_WIDTH
P_MA = P_CX + C_WIDTH
P_MB = P_MA + D_MODEL
P_MC = P_MB + D_MODEL
P_KS = P_MC + D_MODEL
P_VS = P_KS + GW
P_KW = P_VS + GW
P_VW = P_KW + GW
P_CKV = P_VW + GW
P_KC = P_CKV + B_KV_LORA
P_VC = P_KC + A_KV
P_AG = P_VC + A_KV
P_PE = P_AG + SLOT
P_CQ = 21 * B_Q_LORA
P_N = P_CQ + B_Q_LORA
assert P_PE + SLOT <= P_CQ


def _cparams(sem):
    return pltpu.CompilerParams(dimension_semantics=sem, vmem_limit_bytes=VMEM_LIMIT)


def _dot(a, b):
    return jnp.dot(a, b, preferred_element_type=F32)


def _dot_nt(a, b):
    return lax.dot_general(a, b, (((1,), (1,)), ((), ())), preferred_element_type=F32)


def _norm_matmul_kernel(x_ref, g_ref, w_ref, o_ref, h_ref):
    @pl.when(pl.program_id(1) == 0)
    def _():
        x = x_ref[...]
        inv = lax.rsqrt(jnp.mean(x * x, axis=-1, keepdims=True) + NORM_EPS)
        h_ref[...] = ((x * inv) * g_ref[...]).astype(BF16)

    o_ref[...] = _dot(h_ref[...], w_ref[...]).astype(o_ref.dtype)


def norm_matmul(x2d, g, w, tm=1024, tn=1408):
    t, k = x2d.shape
    n = w.shape[1]
    assert n % tn == 0 and tn % LANE == 0
    return pl.pallas_call(
        _norm_matmul_kernel,
        grid=(t // tm, n // tn),
        in_specs=[pl.BlockSpec((tm, k), lambda i, j: (i, 0)),
                  pl.BlockSpec((1, k), lambda i, j: (0, 0)),
                  pl.BlockSpec((k, tn), lambda i, j: (0, j))],
        out_specs=pl.BlockSpec((tm, tn), lambda i, j: (i, j)),
        out_shape=jax.ShapeDtypeStruct((t, n), BF16),
        scratch_shapes=[pltpu.VMEM((tm, k), BF16)],
        compiler_params=_cparams(("parallel", "arbitrary")),
        name="norm_matmul",
    )(x2d, g.reshape(1, k), w)


def _dot_split(x, m):
    hi = x.astype(BF16)
    lo = (x - hi.astype(F32)).astype(BF16)
    return _dot(hi, m) + _dot(lo, m)


def _slot_rms(x, g, width):
    x = x.astype(F32)
    ms = _dot((x * x).astype(BF16), jnp.ones((SLOT, SLOT), BF16)) * (1.0 / width)
    return (x * lax.rsqrt(ms + NORM_EPS)) * g


def _rot_matrix(half, lo):
    r = lax.broadcasted_iota(jnp.int32, (SLOT, SLOT), 0)
    c = lax.broadcasted_iota(jnp.int32, (SLOT, SLOT), 1)
    minus = jnp.logical_and(jnp.logical_and(c >= lo, c < lo + half), r == c + half)
    plus = jnp.logical_and(jnp.logical_and(c >= lo + half, c < lo + 2 * half), r == c - half)
    return jnp.where(minus, -1.0, jnp.where(plus, 1.0, 0.0)).astype(BF16)


def _slot_rope(y, cos, sin, rot):
    return y * cos + _dot_split(y, rot) * sin


def _value_t(v, real):
    vt = v.astype(F32).T
    row = lax.broadcasted_iota(jnp.int32, vt.shape, 0)
    return jnp.where(row == real, 1.0, vt).astype(BF16)


def _nsa_prep_kernel(aq_ref, ks_ref, vs_ref, kw_ref, vw_ref, cos_ref, sin_ref, qg_ref, kg_ref,
                     qn_o, qr_o, ks_o, vs_o, kw_o, vw_o, *, tm):
    cos, sin = cos_ref[...], sin_ref[...]
    qg, kg = qg_ref[...], kg_ref[...]
    scale = A_HEAD_DIM ** -0.5 * LOG2E
    rot = _rot_matrix(A_HEAD_DIM // 2, 0)
    for h in range(A_HEADS):
        sl = slice(h * SLOT, (h + 1) * SLOT)
        y = _slot_rms(aq_ref[:, sl], qg, A_HEAD_DIM)
        qn_o[:, sl] = (y * scale).astype(BF16)
        qr_o[:, sl] = (_slot_rope(y, cos, sin, rot) * scale).astype(BF16)
    t = pl.program_id(1) * tm + lax.broadcasted_iota(jnp.int32, (tm, SLOT), 0)
    lane = lax.broadcasted_iota(jnp.int32, (tm, SLOT), 1)
    blk_bias = jnp.where(lane == A_HEAD_DIM + t // SEL_LEN, NEG_INF, 0.0)
    for g in range(A_GROUPS):
        sl = slice(g * SLOT, (g + 1) * SLOT)
        ksel = _slot_rope(_slot_rms(ks_ref[:, sl], kg, A_HEAD_DIM), cos, sin, rot)
        ks_o[:, sl] = (ksel + blk_bias).astype(BF16)
        kw_o[:, sl] = _slot_rope(_slot_rms(kw_ref[:, sl], kg, A_HEAD_DIM), cos, sin, rot).astype(BF16)
        vs_o[sl, :] = _value_t(vs_ref[:, sl], A_HEAD_DIM)
        vw_o[sl, :] = _value_t(vw_ref[:, sl], A_HEAD_DIM)


def nsa_prep(pa, tabs, qg, kg, tm=512):
    bsz, s, _ = pa.shape
    gw = A_GROUPS * SLOT
    assert s // SEL_LEN <= SLOT - A_HEAD_DIM, "selection blocks must fit the free lanes of a head slot"

    def col(width, off):
        return pl.BlockSpec((None, tm, width), lambda b, i: (b, i, off // width))

    tab = pl.BlockSpec((None, tm, LANE), lambda b, i: (b, i, 0))
    vec = pl.BlockSpec((1, LANE), lambda b, i: (0, 0))
    out_q = pl.BlockSpec((None, tm, A_HEADS * SLOT), lambda b, i: (b, i, 0))
    out_k = pl.BlockSpec((None, tm, gw), lambda b, i: (b, i, 0))
    out_vt = pl.BlockSpec((None, gw, tm), lambda b, i: (b, 0, i))
    sq = jax.ShapeDtypeStruct((bsz, s, A_HEADS * SLOT), BF16)
    sk = jax.ShapeDtypeStruct((bsz, s, gw), BF16)
    svt = jax.ShapeDtypeStruct((bsz, gw, s), BF16)
    return pl.pallas_call(
        functools.partial(_nsa_prep_kernel, tm=tm),
        grid=(bsz, s // tm),
        in_specs=[col(A_HEADS * SLOT, P_AQ), col(gw, P_KS), col(gw, P_VS), col(gw, P_KW), col(gw, P_VW),
                  tab, tab, vec, vec],
        out_specs=[out_q, out_q, out_k, out_vt, out_k, out_vt],
        out_shape=[sq, sq, sk, svt, sk, svt],
        compiler_params=_cparams(("parallel", "parallel")),
        name="nsa_prep",
    )(pa, pa, pa, pa, pa, *tabs, qg, kg)


def _compress_kernel(uk_ref, uv_ref, w1_ref, w2_ref, pos_ref, kg_ref, kc_o, vc_o):
    half = CMP_STRIDE * A_HEAD_DIM
    n = uk_ref.shape[0]
    for z, (u_ref, o_ref) in enumerate(((uk_ref, kc_o), (uv_ref, vc_o))):
        u = u_ref[...].astype(BF16)
        first = _dot(u, w1_ref[z, :half, :])
        second = _dot(u, w1_ref[z, half:, :])
        posc = _dot(pos_ref[z], w1_ref[z])[0:1, :]
        hid = first + pltpu.roll(second, n - 1, 0) + posc
        hid = hid * jax.nn.sigmoid(hid)
        comp = _dot(hid.astype(BF16), w2_ref[z])
        if z == 0:
            ms = jnp.sum(comp * comp, axis=-1, keepdims=True) * (1.0 / A_HEAD_DIM)
            o_ref[...] = ((comp * lax.rsqrt(ms + NORM_EPS)) * kg_ref[...]).astype(BF16)
        else:
            o_ref[...] = comp.astype(BF16)


def nsa_compress(uk, uv, w1, w2, pos, kg):
    bsz, g, n, f = uk.shape
    u_spec = pl.BlockSpec((None, None, n, f), lambda b, gi: (b, gi, 0, 0))
    o_spec = pl.BlockSpec((None, None, n, SLOT), lambda b, gi: (b, gi, 0, 0))
    so = jax.ShapeDtypeStruct((bsz, g, n, SLOT), BF16)
    return pl.pallas_call(
        _compress_kernel,
        grid=(bsz, g),
        in_specs=[u_spec, u_spec,
                  pl.BlockSpec(w1.shape, lambda b, gi: (0, 0, 0)),
                  pl.BlockSpec(w2.shape, lambda b, gi: (0, 0, 0)),
                  pl.BlockSpec(pos.shape, lambda b, gi: (0, 0, 0)),
                  pl.BlockSpec((1, LANE), lambda b, gi: (0, 0))],
        out_specs=[o_spec, o_spec],
        out_shape=[so, so],
        compiler_params=_cparams(("parallel", "parallel")),
        name="nsa_compress",
    )(uk, uv, w1, w2, pos, kg)


def _cmp_attn_kernel(q_ref, kc_ref, vc_ref, ovl_ref, oc_o, sel_o, vct_s, *, tq):
    t0 = pl.program_id(2) * tq
    n_cmp = kc_ref.shape[0]
    n_sel = ovl_ref.shape[0]
    n = A_HPG * tq
    qst = jnp.concatenate([q_ref[:, h * SLOT:(h + 1) * SLOT] for h in range(A_HPG)], axis=0)
    blk = lax.broadcasted_iota(jnp.int32, (n_cmp, n), 0)
    t = t0 + (lax.broadcasted_iota(jnp.int32, (n_cmp, n), 1) & (tq - 1))
    s = jnp.where(t >= blk * CMP_STRIDE + (CMP_LEN - 1), _dot_nt(kc_ref[...], qst), NEG_INF)
    p = jnp.exp2(s - jnp.max(s, axis=0, keepdims=True))
    vct_s[...] = _value_t(vc_ref[...], A_HEAD_DIM)
    acc = _dot(vct_s[...], p.astype(BF16))
    inv_l = 1.0 / acc[A_HEAD_DIM:A_HEAD_DIM + 1, :]
    tq_row = t0 + (lax.broadcasted_iota(jnp.int32, (1, n), 1) & (tq - 1))
    inv_l = jnp.where(tq_row >= CMP_LEN - 1, inv_l, 0.0)
    out = acc * inv_l
    p = p * inv_l
    psum = p[:, 0:tq]
    for h in range(A_HPG):
        oc_o[:, h * SLOT:(h + 1) * SLOT] = out[:, h * tq:(h + 1) * tq].T.astype(oc_o.dtype)
        if h:
            psum = psum + p[:, h * tq:(h + 1) * tq]
    imp = jnp.dot(ovl_ref[...], psum, precision=lax.Precision.HIGHEST, preferred_element_type=F32)
    j = lax.broadcasted_iota(jnp.int32, (n_sel, tq), 0)
    cur = (t0 + lax.broadcasted_iota(jnp.int32, (n_sel, tq), 1)) // SEL_LEN
    big = -NEG_INF
    imp = jnp.where(j == 0, 3.0 * big, jnp.where(j == cur, 2.0 * big, jnp.where(j == cur - 1, big, imp)))
    imp = jnp.where(j > cur, NEG_INF, imp)
    work = imp
    kth = None
    for _ in range(min(SEL_TOPK, n_sel)):
        kth = jnp.max(work, axis=0, keepdims=True)
        work = jnp.where(work >= kth, -jnp.inf, work)
    notsel = jnp.where(jnp.logical_and(imp >= kth, j <= cur), 0.0, 1.0)
    free = SLOT - A_HEAD_DIM
    parts = [jnp.zeros((A_HEAD_DIM, tq), F32), notsel]
    if n_sel < free:
        parts.append(jnp.zeros((free - n_sel, tq), F32))
    sel_o[...] = jnp.concatenate(parts, axis=0).T.astype(sel_o.dtype)


def nsa_cmp_attn(qn, kc, vc, ovl_t, tq=512):
    bsz, s, _ = qn.shape
    n_cmp = kc.shape[2]
    n_sel = ovl_t.shape[0]
    gq = A_HPG * SLOT
    assert tq & (tq - 1) == 0
    kv_spec = pl.BlockSpec((None, None, n_cmp, SLOT), lambda b, g, i: (b, g, 0, 0))
    return pl.pallas_call(
        functools.partial(_cmp_attn_kernel, tq=tq),
        grid=(bsz, A_GROUPS, s // tq),
        in_specs=[pl.BlockSpec((None, tq, gq), lambda b, g, i: (b, i, g)),
                  kv_spec, kv_spec,
                  pl.BlockSpec(ovl_t.shape, lambda b, g, i: (0, 0))],
        out_specs=[pl.BlockSpec((None, tq, gq), lambda b, g, i: (b, i, g)),
                   pl.BlockSpec((None, None, tq, SLOT), lambda b, g, i: (b, g, i, 0))],
        out_shape=[jax.ShapeDtypeStruct((bsz, s, A_HEADS * SLOT), BF16),
                   jax.ShapeDtypeStruct((bsz, A_GROUPS, s, SLOT), BF16)],
        scratch_shapes=[pltpu.VMEM((SLOT, n_cmp), BF16)],
        compiler_params=_cparams(("parallel", "parallel", "parallel")),
        name="nsa_cmp_attn",
    )(qn, kc, vc, ovl_t)


def _softmax_tile(s, m_prev, acc_prev, vt):
    m_new = jnp.maximum(m_prev, jnp.max(s, axis=0, keepdims=True))
    alpha = jnp.exp2(m_prev - m_new)
    p = jnp.exp2(s - m_new).astype(BF16)
    return m_new, alpha * acc_prev + _dot(vt, p)


def _visible(kind, k0, q0, tk, tq, n):
    key = k0 + lax.broadcasted_iota(jnp.int32, (tk, n), 0)
    qry = q0 + (lax.broadcasted_iota(jnp.int32, (tk, n), 1) & (tq - 1))
    return key <= qry if kind == "causal" else key > qry - WINDOW


def _store_heads(o_ref, b, hpg, tq, acc, real_v):
    out = acc / acc[real_v:real_v + 1, :]
    for h in range(hpg):
        o_ref[:, (b * hpg + h) * SLOT:(b * hpg + h + 1) * SLOT] = out[:, h * tq:(h + 1) * tq].T.astype(o_ref.dtype)


def _flash_kernel(qi_ref, kj_ref, fl_ref, *refs, hb, hpg, real_v, select, tq, tk):
    if select:
        q_ref, k_ref, vt_ref, nsel_ref, o_ref, qst_s, m_s, acc_s = refs
    else:
        q_ref, k_ref, vt_ref, o_ref, qst_s, m_s, acc_s = refs
    step_id = pl.program_id(2)
    q0 = qi_ref[step_id] * tq
    k0 = kj_ref[step_id] * tk
    flags = fl_ref[step_id]
    n = hpg * tq

    @pl.when((flags & STEP_FIRST) != 0)
    def _():
        for b in range(hb):
            for h in range(hpg):
                qh = q_ref[:, (b * hpg + h) * SLOT:(b * hpg + h + 1) * SLOT]
                if select:
                    qh = qh + nsel_ref[b]
                qst_s[b, h * tq:(h + 1) * tq, :] = qh
        m_s[...] = jnp.full(m_s.shape, NEG_INF, F32)
        acc_s[...] = jnp.zeros(acc_s.shape, F32)

    def step(kind):
        visible = _visible(kind, k0, q0, tk, tq, n) if kind != "none" else None
        for b in range(hb):
            s = _dot_nt(k_ref[:, b * SLOT:(b + 1) * SLOT], qst_s[b])
            if visible is not None:
                s = jnp.where(visible, s, NEG_INF)
            m_s[b], acc_s[b] = _softmax_tile(s, m_s[b], acc_s[b], vt_ref[b * SLOT:(b + 1) * SLOT, :])

    @pl.when((flags & STEP_DIAG) == 0)
    def _():
        step("none")

    @pl.when((flags & STEP_DIAG) != 0)
    def _():
        step("causal")

    @pl.when((flags & STEP_LAST) != 0)
    def _():
        for b in range(hb):
            _store_heads(o_ref, b, hpg, tq, acc_s[b], real_v)


STEP_FIRST, STEP_LAST, STEP_DIAG = 1, 2, 4


def _flash_steps(n_q, tq, tk):
    qi, kj, fl = [], [], []
    for i in range(n_q):
        last = (i * tq + tq - 1) // tk
        for j in range(last + 1):
            diag = j * tk + tk - 1 > i * tq
            qi.append(i)
            kj.append(j)
            fl.append((STEP_FIRST if j == 0 else 0) | (STEP_LAST if j == last else 0) | (STEP_DIAG if diag else 0))
    return [jnp.asarray(np.asarray(a, np.int32)) for a in (qi, kj, fl)]


def flash_attention(q, k, vt, nsel=None, *, hb, hpg, real_v, tq, tk):
    bsz, s, qw = q.shape
    groups = qw // (hpg * SLOT)
    assert tq & (tq - 1) == 0 and groups % hb == 0
    tables = _flash_steps(s // tq, tq, tk)
    n_steps = tables[0].shape[0]
    in_specs = [pl.BlockSpec((None, tq, hb * hpg * SLOT), lambda b, g, t, qi, kj, fl: (b, qi[t], g)),
                pl.BlockSpec((None, tk, hb * SLOT), lambda b, g, t, qi, kj, fl: (b, kj[t], g)),
                pl.BlockSpec((None, hb * SLOT, tk), lambda b, g, t, qi, kj, fl: (b, g, kj[t]))]
    args = [q, k, vt]
    if nsel is not None:
        in_specs.append(pl.BlockSpec((None, hb, tq, SLOT), lambda b, g, t, qi, kj, fl: (b, g, qi[t], 0)))
        args.append(nsel)
    return pl.pallas_call(
        functools.partial(_flash_kernel, hb=hb, hpg=hpg, real_v=real_v, select=nsel is not None, tq=tq, tk=tk),
        grid_spec=pltpu.PrefetchScalarGridSpec(
            num_scalar_prefetch=3,
            grid=(bsz, groups // hb, n_steps),
            in_specs=in_specs,
            out_specs=pl.BlockSpec((None, tq, hb * hpg * SLOT), lambda b, g, t, qi, kj, fl: (b, qi[t], g)),
            scratch_shapes=[pltpu.VMEM((hb, hpg * tq, SLOT), BF16), pltpu.VMEM((hb, 1, hpg * tq), F32),
                            pltpu.VMEM((hb, SLOT, hpg * tq), F32)]),
        out_shape=jax.ShapeDtypeStruct((bsz, s, qw), BF16),
        compiler_params=_cparams(("parallel", "parallel", "arbitrary")),
        name="flash_select" if nsel is not None else "flash_causal",
    )(*tables, *args)


def _window_kernel(q_ref, *refs, hb, hpg, real_v, tq, nk):
    k_refs, vt_refs, o_ref = refs[:nk], refs[nk:2 * nk], refs[2 * nk]
    i = pl.program_id(2)
    q0 = i * tq
    n = hpg * tq
    for b in range(hb):
        qst = jnp.concatenate([q_ref[:, (b * hpg + h) * SLOT:(b * hpg + h + 1) * SLOT] for h in range(hpg)], axis=0)
        m = jnp.full((1, n), NEG_INF, F32)
        acc = jnp.zeros((SLOT, n), F32)
        for t in range(nk):
            back = nk - 1 - t
            s = _dot_nt(k_refs[t][:, b * SLOT:(b + 1) * SLOT], qst)
            if t == 0:
                s = jnp.where(jnp.logical_and(_visible("lower", q0 - back * tq, q0, tq, tq, n), i >= back), s, NEG_INF)
            elif t == nk - 1:
                s = jnp.where(_visible("causal", q0, q0, tq, tq, n), s, NEG_INF)
            else:
                s = jnp.where(i >= back, s, NEG_INF)
            m, acc = _softmax_tile(s, m, acc, vt_refs[t][b * SLOT:(b + 1) * SLOT, :])
        _store_heads(o_ref, b, hpg, tq, acc, real_v)


def window_attention(q, k, vt, *, hb, hpg, real_v, tq):
    bsz, s, qw = q.shape
    groups = qw // (hpg * SLOT)
    assert tq & (tq - 1) == 0 and groups % hb == 0 and WINDOW % tq == 0
    nk = WINDOW // tq + 1

    def k_spec(t):
        return pl.BlockSpec((None, tq, hb * SLOT), lambda b, g, i: (b, jnp.maximum(i - (nk - 1 - t), 0), g))

    def vt_spec(t):
        return pl.BlockSpec((None, hb * SLOT, tq), lambda b, g, i: (b, g, jnp.maximum(i - (nk - 1 - t), 0)))

    q_spec = pl.BlockSpec((None, tq, hb * hpg * SLOT), lambda b, g, i: (b, i, g))
    return pl.pallas_call(
        functools.partial(_window_kernel, hb=hb, hpg=hpg, real_v=real_v, tq=tq, nk=nk),
        grid=(bsz, groups // hb, s // tq),
        in_specs=[q_spec] + [k_spec(t) for t in range(nk)] + [vt_spec(t) for t in range(nk)],
        out_specs=q_spec,
        out_shape=jax.ShapeDtypeStruct((bsz, s, qw), BF16),
        compiler_params=_cparams(("parallel", "parallel", "parallel")),
        name="flash_window",
    )(q, *([k] * nk), *([vt] * nk))


def _mla_prep_kernel(ckv_ref, kpe_ref, cq_ref, cqg_ref, ckvg_ref, wuq_ref, wuk_ref, wuv_ref,
                     cos_ref, sin_ref, qg_ref, kg_ref, q_o, k_o, v_o):
    def rms(x, g):
        x = x.astype(F32)
        return ((x * lax.rsqrt(jnp.mean(x * x, axis=-1, keepdims=True) + NORM_EPS)) * g).astype(BF16)

    cq = rms(cq_ref[...], cqg_ref[...])
    ckv = rms(ckv_ref[...], ckvg_ref[...])
    q = _dot(cq, wuq_ref[...])
    kn = _dot(ckv, wuk_ref[...])
    vv = _dot(ckv, wuv_ref[...])
    kpe = kpe_ref[...].astype(F32)
    cos, sin = cos_ref[...], sin_ref[...]
    qg, kg = qg_ref[...], kg_ref[...]
    scale = B_QK ** -0.5
    rot = _rot_matrix(B_ROPE // 2, B_NOPE)
    for h in range(B_HEADS):
        sl = slice(h * SLOT, (h + 1) * SLOT)
        yq = _slot_rope(_slot_rms(q[:, sl], qg, B_QK), cos, sin, rot)
        q_o[:, sl] = (yq * (scale * LOG2E)).astype(BF16)
        yk = _slot_rope(_slot_rms(kn[:, sl] + kpe, kg, B_QK), cos, sin, rot)
        k_o[:, sl] = yk.astype(BF16)
        v_o[sl, :] = _value_t(vv[:, sl], B_V)


def mla_prep(pb, cqg, ckvg, wuq, wuk, wuv, tabs, qg, kg, tm=512):
    bsz, s, _ = pb.shape
    hw = B_HEADS * SLOT

    def const(shape):
        return pl.BlockSpec(shape, lambda b, i: (0,) * len(shape))

    tab = pl.BlockSpec((None, tm, LANE), lambda b, i: (b, i, 0))
    out = pl.BlockSpec((None, tm, hw), lambda b, i: (b, i, 0))
    so = jax.ShapeDtypeStruct((bsz, s, hw), BF16)
    return pl.pallas_call(
        _mla_prep_kernel,
        grid=(bsz, s // tm),
        in_specs=[pl.BlockSpec((None, tm, B_KV_LORA), lambda b, i: (b, i, P_CKV // B_KV_LORA)),
                  pl.BlockSpec((None, tm, SLOT), lambda b, i: (b, i, P_PE // SLOT)),
                  pl.BlockSpec((None, tm, B_Q_LORA), lambda b, i: (b, i, P_CQ // B_Q_LORA)),
                  const((1, B_Q_LORA)), const((1, B_KV_LORA)),
                  const(wuq.shape), const(wuk.shape), const(wuv.shape),
                  tab, tab, const((1, LANE)), const((1, LANE))],
        out_specs=[out, out, pl.BlockSpec((None, hw, tm), lambda b, i: (b, 0, i))],
        out_shape=[so, so, jax.ShapeDtypeStruct((bsz, hw, s), BF16)],
        compiler_params=_cparams(("parallel", "parallel")),
        name="mla_prep",
    )(pb, pb, pb, cqg, ckvg, wuq, wuk, wuv, *tabs, qg, kg)


SUBLANES = 8


def _rglru_kernel(cg_ref, cx_ref, cw_ref, cb_ref, wa_ref, ba_ref, wx_ref, bx_ref, lam_ref, o_ref,
                  xbuf, a_s, b_s, carry, *, ts):
    @pl.when(pl.program_id(1) == 0)
    def _():
        xbuf[0:SUBLANES, :] = jnp.zeros((SUBLANES, C_WIDTH), F32)
        carry[...] = jnp.zeros(carry.shape, F32)

    xr = cx_ref[...].astype(F32)
    xbuf[SUBLANES:SUBLANES + ts, :] = xr
    xc = cb_ref[...] + xr * cw_ref[C_CONV - 1:C_CONV, :]
    for w in range(C_CONV - 1):
        back = C_CONV - 1 - w
        xc = xc + xbuf[pl.ds(SUBLANES - back, ts), :] * cw_ref[w:w + 1, :]
    xbuf[0:SUBLANES, :] = xr[ts - SUBLANES:ts, :]

    for n in range(C_BLOCKS):
        sl = slice(n * C_BLOCK_W, (n + 1) * C_BLOCK_W)
        xb = xc[:, sl].astype(BF16)
        a_s[:, sl] = _dot(xb, wa_ref[n])
        b_s[:, sl] = _dot(xb, wx_ref[n])
    r = jax.nn.sigmoid(a_s[...] + ba_ref[...])
    ig = jax.nn.sigmoid(b_s[...] + bx_ref[...])
    nlam = -lam_ref[...]
    softplus = jnp.maximum(nlam, 0.0) + jnp.log1p(jnp.exp(-jnp.abs(nlam)))
    log_a = (-C_SCALE) * r * softplus
    a_s[...] = jnp.exp(log_a)
    th = jnp.tanh(log_a)
    b_s[...] = jnp.sqrt((-2.0 * th) / (1.0 - th)) * (ig * xc)

    row = lax.broadcasted_iota(jnp.int32, (SUBLANES, C_WIDTH), 0)

    def group(gi, h_prev):
        off = pl.multiple_of(gi * SUBLANES, SUBLANES)
        a = a_s[pl.ds(off, SUBLANES), :]
        b = b_s[pl.ds(off, SUBLANES), :]
        for d in (1, 2, 4):
            a_sh = pltpu.roll(a, d, 0)
            b_sh = pltpu.roll(b, d, 0)
            keep = row >= d
            b = jnp.where(keep, a * b_sh + b, b)
            a = jnp.where(keep, a * a_sh, a)
        h = b + a * h_prev
        b_s[pl.ds(off, SUBLANES), :] = h
        return h[SUBLANES - 1:SUBLANES, :]

    carry[...] = lax.fori_loop(0, ts // SUBLANES, group, carry[...])
    gate = cg_ref[...].astype(F32)
    gelu =0.5 * gate * (1.0 + jnp.tanh(0.7978845608028654 * (gate + 0.044715 * (gate * gate * gate))))
    o_ref[...] = (gelu * b_s[...]).astype(o_ref.dtype)


def rglru(pc, cw, cb, wa, ba, wx, bx, lam, ts=256):
    bsz, s, _ = pc.shape

    def const(shape):
        return pl.BlockSpec(shape, lambda b, i: (0,) * len(shape))

    vec = const((1, C_WIDTH))
    return pl.pallas_call(
        functools.partial(_rglru_kernel, ts=ts),
        grid=(bsz, s // ts),
        in_specs=[pl.BlockSpec((None, ts, C_WIDTH), lambda b, i: (b, i, P_CG // C_WIDTH)),
                  pl.BlockSpec((None, ts, C_WIDTH), lambda b, i: (b, i, P_CX // C_WIDTH)),
                  const((C_CONV, C_WIDTH)), vec, const(wa.shape), vec, const(wx.shape), vec, vec],
        out_specs=pl.BlockSpec((None, ts, C_WIDTH), lambda b, i: (b, i, 0)),
        out_shape=jax.ShapeDtypeStruct((bsz, s, C_WIDTH), BF16),
        scratch_shapes=[pltpu.VMEM((ts + SUBLANES, C_WIDTH), F32), pltpu.VMEM((ts, C_WIDTH), F32),
                        pltpu.VMEM((ts, C_WIDTH), F32), pltpu.VMEM((1, C_WIDTH), F32)],
        compiler_params=_cparams(("parallel", "arbitrary")),
        name="rglru",
    )(pc, pc, cw, cb, wa, ba, wx, bx, lam)


def _merge_kernel(oc_ref, os_ref, ow_ref, ag_ref, yb_ref, yc_ref, ma_ref, mb_ref, mc_ref, x_ref,
                  wpa_ref, wpb_ref, wpc_ref, wo_ref, o_ref, ya_s):
    gates = jax.nn.sigmoid(ag_ref[...].astype(F32))
    for h in range(A_HEADS):
        sl = slice(h * SLOT, (h + 1) * SLOT)
        y = (gates[:, 3 * h:3 * h + 1] * oc_ref[:, sl].astype(F32)
             + gates[:, 3 * h + 1:3 * h + 2] * os_ref[:, sl].astype(F32)
             + gates[:, 3 * h + 2:3 * h + 3] * ow_ref[:, sl].astype(F32))
        ya_s[:, sl] = y.astype(BF16)
    merged = (jax.nn.sigmoid(ma_ref[...].astype(F32)) * _dot(ya_s[...], wpa_ref[...])
              + jax.nn.sigmoid(mb_ref[...].astype(F32)) * _dot(yb_ref[...], wpb_ref[...])
              + jax.nn.sigmoid(mc_ref[...].astype(F32)) * _dot(yc_ref[...], wpc_ref[...]))
    o_ref[...] = x_ref[...] + _dot(merged.astype(BF16), wo_ref[...])


def merge(oc, osel, ow, pa2d, yb, yc, pm, x2d, wpa, wpb, wpc, wo, tm=512):
    t = x2d.shape[0]
    d = D_MODEL

    def rows(width, off=0):
        return pl.BlockSpec((tm, width), lambda i: (i, off // width))

    w_spec = pl.BlockSpec((d, d), lambda i: (0, 0), pipeline_mode=pl.Buffered(1))
    return pl.pallas_call(
        _merge_kernel,
        grid=(t // tm,),
        in_specs=[rows(d), rows(d), rows(d), rows(SLOT, P_AG), rows(d), rows(d),
                  rows(d, P_MA), rows(d, P_MB), rows(d, P_MC), rows(d),
                  w_spec, w_spec, w_spec, w_spec],
        out_specs=rows(d),
        out_shape=jax.ShapeDtypeStruct((t, d), F32),
        scratch_shapes=[pltpu.VMEM((tm, d), BF16)],
        compiler_params=_cparams(("parallel",)),
        name="merge",
    )(oc, osel, ow, pa2d, yb, yc, pm, pm, pm, x2d, wpa, wpb, wpc, wo)


def _ffn_kernel(x_ref, g_ref, w1_ref, w3_ref, w2_ref, o_ref, h_s, acc_s, *, nf):
    f = pl.program_id(1)

    @pl.when(f == 0)
    def _():
        x = x_ref[...]
        inv = lax.rsqrt(jnp.mean(x * x, axis=-1, keepdims=True) + NORM_EPS)
        h_s[...] = ((x * inv) * g_ref[...]).astype(BF16)
        acc_s[...] = jnp.zeros(acc_s.shape, F32)

    h = h_s[...]
    u = _dot(h, w1_ref[...])
    z = (u * jax.nn.sigmoid(u)) * _dot(h, w3_ref[...])
    acc_s[...] += _dot(z.astype(BF16), w2_ref[...])

    @pl.when(f == nf - 1)
    def _():
        o_ref[...] = x_ref[...] + acc_s[...]


def ffn(x2d, g, w1, w3, w2, tm=1024, tf=256):
    t, d = x2d.shape
    hidden = w1.shape[1]
    nf = hidden // tf
    return pl.pallas_call(
        functools.partial(_ffn_kernel, nf=nf),
        grid=(t // tm, nf),
        in_specs=[pl.BlockSpec((tm, d), lambda i, f: (i, 0)),
                  pl.BlockSpec((1, d), lambda i, f: (0, 0)),
                  pl.BlockSpec((d, tf), lambda i, f: (0, f)),
                  pl.BlockSpec((d, tf), lambda i, f: (0, f)),
                  pl.BlockSpec((tf, d), lambda i, f: (f, 0))],
        out_specs=pl.BlockSpec((tm, d), lambda i, f: (i, 0)),
        out_shape=jax.ShapeDtypeStruct((t, d), F32),
        scratch_shapes=[pltpu.VMEM((tm, d), BF16), pltpu.VMEM((tm, d), F32)],
        compiler_params=_cparams(("parallel", "arbitrary")),
        name="ffn",
    )(x2d, g.reshape(1, d), w1, w3, w2)


def _pad_cols(w, width):
    return jnp.pad(w, ((0, 0), (0, width - w.shape[1])))


def _slot_cols(w, heads, real):
    k = w.shape[0]
    return jnp.pad(w.reshape(k, heads, real), ((0, 0), (0, 0), (0, SLOT - real))).reshape(k, heads * SLOT)


def _slot_rows(w, heads, real):
    n = w.shape[1]
    return jnp.pad(w.reshape(heads, real, n), ((0, 0), (0, SLOT - real), (0, 0))).reshape(heads * SLOT, n)


def _split_w_in(w):
    sizes = (A_Q, A_KV, A_KV, A_KV, A_KV, A_KV, A_KV, A_GATES, B_Q_LORA, B_KV_LORA, B_ROPE,
             C_WIDTH, C_WIDTH, D_MODEL, D_MODEL, D_MODEL)
    offs = np.cumsum((0,) + sizes)
    (aq, akc, avc, aks, avs, akw, avw, ag, bcq, bckv, bkpe, cg, cx, ma, mb, mc) = [
        w[:, offs[n]:offs[n + 1]] for n in range(len(sizes))]
    kpe_slot = jnp.pad(bkpe, ((0, 0), (B_NOPE, SLOT - B_NOPE - B_ROPE)))
    pieces = [(P_AQ, _slot_cols(aq, A_HEADS, A_HEAD_DIM)), (P_CG, cg), (P_CX, cx), (P_MA, ma), (P_MB, mb),
              (P_MC, mc), (P_KS, _slot_cols(aks, A_GROUPS, A_HEAD_DIM)), (P_VS, _slot_cols(avs, A_GROUPS, A_HEAD_DIM)),
              (P_KW, _slot_cols(akw, A_GROUPS, A_HEAD_DIM)), (P_VW, _slot_cols(avw, A_GROUPS, A_HEAD_DIM)),
              (P_CKV, bckv), (P_KC, akc), (P_VC, avc), (P_AG, _pad_cols(ag, SLOT)), (P_PE, kpe_slot), (P_CQ, bcq)]
    cols, at = [], 0
    for off, piece in pieces:
        assert off >= at
        if off > at:
            cols.append(jnp.zeros((w.shape[0], off - at), w.dtype))
        cols.append(piece)
        at = off + piece.shape[1]
    assert at == P_N
    return jnp.concatenate(cols, axis=1).astype(BF16)


def _rope_tables(positions, half, lo):
    inv = ROPE_THETA ** (-jnp.arange(half, dtype=F32) / half)
    ang = positions.astype(F32)[..., None] * inv
    cos, sin = jnp.cos(ang), jnp.sin(ang)
    shape = ang.shape[:-1]
    tail = SLOT - lo - 2 * half
    cos_t = jnp.concatenate([jnp.ones(shape + (lo,), F32), cos, cos, jnp.ones(shape + (tail,), F32)], axis=-1)
    sin_t = jnp.concatenate([jnp.zeros(shape + (lo,), F32), sin, sin, jnp.zeros(shape + (tail,), F32)], axis=-1)
    return cos_t, sin_t


def _overlap_t(s):
    n_cmp = s // CMP_STRIDE
    n_sel = s // SEL_LEN
    starts = np.arange(n_cmp) * CMP_STRIDE
    sel_starts = np.arange(n_sel) * SEL_LEN
    ovl = np.clip(np.minimum(starts[:, None] + CMP_LEN, sel_starts[None, :] + SEL_LEN)
                  - np.maximum(starts[:, None], sel_starts[None, :]), 0, None) / CMP_LEN
    ovl[(s - CMP_LEN) // CMP_STRIDE + 1:, :] = 0.0
    return jnp.asarray(ovl.T, dtype=F32)


def _vec_slot(g):
    return jnp.pad(g, (0, SLOT - g.shape[0])).reshape(1, SLOT).astype(F32)


def kernel(x, positions, mix_norm_g, w_in, a_q_norm_g, a_k_norm_g, a_cmp_pos, a_cmp_w1, a_cmp_w2, b_cq_norm_g, b_ckv_norm_g, b_w_uq, b_w_ukv, b_q_norm_g, b_k_norm_g, c_conv_w, c_conv_b, c_w_a, c_b_a, c_w_x, c_b_x, c_lambda, w_pa, w_pb, w_pc, w_o, ffn_norm_g, ffn_w1, ffn_w3, ffn_w2):
    bsz, s, d = x.shape
    t = bsz * s
    depth = w_in.shape[0]
    n_chunk = s // CMP_STRIDE
    tabs_a = _rope_tables(positions, A_HEAD_DIM // 2, 0)
    tabs_b = _rope_tables(positions, B_ROPE // 2, B_NOPE)
    ovl_t = _overlap_t(s)

    x2d = x.reshape(t, d)
    for l in range(depth):
        pj = norm_matmul(x2d, mix_norm_g[l], _split_w_in(w_in[l]))
        pj3 = pj.reshape(bsz, s, P_N)

        qg, kg = _vec_slot(a_q_norm_g[l]), _vec_slot(a_k_norm_g[l])
        qn, qr, ks, vs, kw, vw = nsa_prep(pj3, tabs_a, qg, kg)

        def chunks(off):
            u = pj3[:, :, off:off + A_KV].reshape(bsz, n_chunk, CMP_STRIDE, A_GROUPS, A_HEAD_DIM)
            return u.transpose(0, 3, 1, 2, 4).reshape(bsz, A_GROUPS, n_chunk, CMP_STRIDE * A_HEAD_DIM)

        w1 = jnp.pad(a_cmp_w1[l], ((0, 0), (0, 0), (0, SLOT - A_HEAD_DIM))).astype(BF16)
        w2 = jnp.pad(a_cmp_w2[l], ((0, 0), (0, SLOT - A_HEAD_DIM), (0, SLOT - A_HEAD_DIM))).astype(BF16)
        pos = jnp.broadcast_to(a_cmp_pos[l].reshape(2, 1, CMP_LEN * A_HEAD_DIM),
                               (2, SUBLANES, CMP_LEN * A_HEAD_DIM)).astype(BF16)
        kc, vc = nsa_compress(chunks(P_KC), chunks(P_VC), w1, w2, pos, kg)
        oc, nsel = nsa_cmp_attn(qn, kc, vc, ovl_t)
        osel = flash_attention(qr, ks, vs, nsel, hb=A_GROUPS, hpg=A_HPG, real_v=A_HEAD_DIM, tq=512, tk=512)
        ow = window_attention(qr, kw, vw, hb=A_GROUPS, hpg=A_HPG, real_v=A_HEAD_DIM, tq=256)

        wuq = _slot_cols(b_w_uq[l], B_HEADS, B_QK).astype(BF16)
        ukv = b_w_ukv[l].reshape(B_KV_LORA, B_HEADS, B_NOPE + B_V)
        wuk = _slot_cols(ukv[:, :, :B_NOPE].reshape(B_KV_LORA, B_HEADS * B_NOPE), B_HEADS, B_NOPE).astype(BF16)
        wuv = _slot_cols(ukv[:, :, B_NOPE:].reshape(B_KV_LORA, B_HEADS * B_V), B_HEADS, B_V).astype(BF16)
        bq, bk, bv = mla_prep(pj3, b_cq_norm_g[l].reshape(1, -1), b_ckv_norm_g[l].reshape(1, -1),
                              wuq, wuk, wuv, tabs_b, _vec_slot(b_q_norm_g[l]), _vec_slot(b_k_norm_g[l]))
        yb = flash_attention(bq, bk, bv, hb=4, hpg=1, real_v=B_V, tq=512, tk=512)

        yc = rglru(pj3, c_conv_w[l], c_conv_b[l].reshape(1, -1),
                   c_w_a[l].astype(BF16), c_b_a[l].reshape(1, -1), c_w_x[l].astype(BF16),
                   c_b_x[l].reshape(1, -1), c_lambda[l].reshape(1, -1))

        x2d = merge(oc.reshape(t, -1), osel.reshape(t, -1), ow.reshape(t, -1), pj,
                    yb.reshape(t, -1), yc.reshape(t, -1), pj, x2d,
                    _slot_rows(w_pa[l], A_HEADS, A_HEAD_DIM).astype(BF16),
                    _slot_rows(w_pb[l], B_HEADS, B_V).astype(BF16),
                    w_pc[l].astype(BF16), w_o[l].astype(BF16))
        x2d = ffn(x2d, ffn_norm_g[l], ffn_w1[l].astype(BF16), ffn_w3[l].astype(BF16), ffn_w2[l].astype(BF16))
    return x2d.reshape(bsz, s, d)
```

```python
import functools

import numpy as np
import jax
import jax.numpy as jnp
from jax import lax
from jax.experimental import pallas as pl
from jax.experimental.pallas import tpu as pltpu

F32 = jnp.float32
BF16 = jnp.bfloat16

D_MODEL = 1024
ROPE_THETA = 10000.0
NORM_EPS = 1e-6
NEG_INF = -1e30
LOG2E = 1.4426950408889634

A_HEADS = 8
A_GROUPS = 2
A_HPG = A_HEADS // A_GROUPS
A_HEAD_DIM = 64
CMP_LEN = 32
CMP_STRIDE = 16
SEL_LEN = 64
SEL_TOPK = 16
WINDOW = 512

B_HEADS = 8
B_Q_LORA = 384
B_KV_LORA = 256
B_NOPE = 64
B_ROPE = 32
B_V = 64
B_QK = B_NOPE + B_ROPE

C_WIDTH = 1024
C_BLOCKS = 8
C_BLOCK_W = C_WIDTH // C_BLOCKS
C_CONV = 4
C_SCALE = 8.0

FFN_HIDDEN = 2816

A_Q = A_HEADS * A_HEAD_DIM
A_KV = A_GROUPS * A_HEAD_DIM
A_GATES = 3 * A_HEADS

LANE = 128
SLOT = LANE
VMEM_LIMIT = 56 * 1024 * 1024

GW = A_GROUPS * SLOT
P_AQ = 0
P_CG = P_AQ + A_HEADS * SLOT
P_CX = P_CG + C_WIDTH
P_MA = P_CX + C_WIDTH
P_MB = P_MA + D_MODEL
P_MC = P_MB + D_MODEL
P_KS = P_MC + D_MODEL
P_VS = P_KS + GW
P_KW = P_VS + GW
P_VW = P_KW + GW
P_CKV = P_VW + GW
P_KC = P_CKV + B_KV_LORA
P_VC = P_KC + A_KV
P_AG = P_VC + A_KV
P_PE = P_AG + SLOT
P_CQ = 21 * B_Q_LORA
P_N = P_CQ + B_Q_LORA
assert P_PE + SLOT <= P_CQ


def _cparams(sem):
    return pltpu.CompilerParams(dimension_semantics=sem, vmem_limit_bytes=VMEM_LIMIT)


def _dot(a, b):
    return jnp.dot(a, b, preferred_element_type=F32)


def _sigmoid(x):
    return 0.5 * jnp.tanh(0.5 * x) + 0.5


def _dot_nt(a, b):
    return lax.dot_general(a, b, (((1,), (1,)), ((), ())), preferred_element_type=F32)


def _norm_matmul_kernel(x_ref, g_ref, w_ref, o_ref, h_ref):
    @pl.when(pl.program_id(1) == 0)
    def _():
        x = x_ref[...]
        inv = lax.rsqrt(jnp.mean(x * x, axis=-1, keepdims=True) + NORM_EPS)
        h_ref[...] = ((x * inv) * g_ref[...]).astype(BF16)

    o_ref[...] = _dot(h_ref[...], w_ref[...]).astype(o_ref.dtype)


def norm_matmul(x2d, g, w, tm=1024, tn=1408):
    t, k = x2d.shape
    n = w.shape[1]
    assert n % tn == 0 and tn % LANE == 0
    return pl.pallas_call(
        _norm_matmul_kernel,
        grid=(t // tm, n // tn),
        in_specs=[pl.BlockSpec((tm, k), lambda i, j: (i, 0)),
                  pl.BlockSpec((1, k), lambda i, j: (0, 0)),
                  pl.BlockSpec((k, tn), lambda i, j: (0, j))],
        out_specs=pl.BlockSpec((tm, tn), lambda i, j: (i, j)),
        out_shape=jax.ShapeDtypeStruct((t, n), BF16),
        scratch_shapes=[pltpu.VMEM((tm, k), BF16)],
        compiler_params=_cparams(("parallel", "arbitrary")),
        name="norm_matmul",
    )(x2d, g.reshape(1, k), w)


def _dot_split(x, m):
    hi = x.astype(BF16)
    lo = (x - hi.astype(F32)).astype(BF16)
    return _dot(hi, m) + _dot(lo, m)


def _slot_rms(x, g, width):
    x = x.astype(F32)
    ms = _dot((x * x).astype(BF16), jnp.ones((SLOT, SLOT), BF16)) * (1.0 / width)
    return (x * lax.rsqrt(ms + NORM_EPS)) * g


def _rot_matrix(half, lo):
    r = lax.broadcasted_iota(jnp.int32, (SLOT, SLOT), 0)
    c = lax.broadcasted_iota(jnp.int32, (SLOT, SLOT), 1)
    minus = jnp.logical_and(jnp.logical_and(c >= lo, c < lo + half), r == c + half)
    plus = jnp.logical_and(jnp.logical_and(c >= lo + half, c < lo + 2 * half), r == c - half)
    return jnp.where(minus, -1.0, jnp.where(plus, 1.0, 0.0)).astype(BF16)


def _slot_rope(y, cos, sin, rot):
    return y * cos + _dot_split(y, rot) * sin


def _value_t(v, real):
    vt = v.astype(F32).T
    row = lax.broadcasted_iota(jnp.int32, vt.shape, 0)
    return jnp.where(row == real, 1.0, vt).astype(BF16)


def _nsa_prep_kernel(aq_ref, ks_ref, vs_ref, kw_ref, vw_ref, cos_ref, sin_ref, qg_ref, kg_ref,
                     qn_o, qr_o, ks_o, vs_o, kw_o, vw_o, *, tm):
    cos, sin = cos_ref[...], sin_ref[...]
    qg, kg = qg_ref[...], kg_ref[...]
    scale = A_HEAD_DIM ** -0.5 * LOG2E
    rot = _rot_matrix(A_HEAD_DIM // 2, 0)
    for h in range(A_HEADS):
        sl = slice(h * SLOT, (h + 1) * SLOT)
        y = _slot_rms(aq_ref[:, sl], qg, A_HEAD_DIM)
        qn_o[:, sl] = (y * scale).astype(BF16)
        qr_o[:, sl] = (_slot_rope(y, cos, sin, rot) * scale).astype(BF16)
    t = pl.program_id(1) * tm + lax.broadcasted_iota(jnp.int32, (tm, SLOT), 0)
    lane = lax.broadcasted_iota(jnp.int32, (tm, SLOT), 1)
    blk_bias = jnp.where(lane == A_HEAD_DIM + t // SEL_LEN, NEG_INF, 0.0)
    for g in range(A_GROUPS):
        sl = slice(g * SLOT, (g + 1) * SLOT)
        ksel = _slot_rope(_slot_rms(ks_ref[:, sl], kg, A_HEAD_DIM), cos, sin, rot)
        ks_o[:, sl] = (ksel + blk_bias).astype(BF16)
        kw_o[:, sl] = _slot_rope(_slot_rms(kw_ref[:, sl], kg, A_HEAD_DIM), cos, sin, rot).astype(BF16)
        vs_o[sl, :] = _value_t(vs_ref[:, sl], A_HEAD_DIM)
        vw_o[sl, :] = _value_t(vw_ref[:, sl], A_HEAD_DIM)


def nsa_prep(pa, tabs, qg, kg, tm=512):
    bsz, s, _ = pa.shape
    gw = A_GROUPS * SLOT
    assert s // SEL_LEN <= SLOT - A_HEAD_DIM, "selection blocks must fit the free lanes of a head slot"

    def col(width, off):
        return pl.BlockSpec((None, tm, width), lambda b, i: (b, i, off // width))

    tab = pl.BlockSpec((None, tm, LANE), lambda b, i: (b, i, 0))
    vec = pl.BlockSpec((1, LANE), lambda b, i: (0, 0))
    out_q = pl.BlockSpec((None, tm, A_HEADS * SLOT), lambda b, i: (b, i, 0))
    out_k = pl.BlockSpec((None, tm, gw), lambda b, i: (b, i, 0))
    out_vt = pl.BlockSpec((None, gw, tm), lambda b, i: (b, 0, i))
    sq = jax.ShapeDtypeStruct((bsz, s, A_HEADS * SLOT), BF16)
    sk = jax.ShapeDtypeStruct((bsz, s, gw), BF16)
    svt = jax.ShapeDtypeStruct((bsz, gw, s), BF16)
    return pl.pallas_call(
        functools.partial(_nsa_prep_kernel, tm=tm),
        grid=(bsz, s // tm),
        in_specs=[col(A_HEADS * SLOT, P_AQ), col(gw, P_KS), col(gw, P_VS), col(gw, P_KW), col(gw, P_VW),
                  tab, tab, vec, vec],
        out_specs=[out_q, out_q, out_k, out_vt, out_k, out_vt],
        out_shape=[sq, sq, sk, svt, sk, svt],
        compiler_params=_cparams(("parallel", "parallel")),
        name="nsa_prep",
    )(pa, pa, pa, pa, pa, *tabs, qg, kg)


def _compress_kernel(kx_ref, vx_ref, w1_ref, w2_ref, pos_ref, kg_ref, kc_o, vc_o, xs):
    n = kx_ref.shape[0]
    r = lax.broadcasted_iota(jnp.int32, (SLOT, SLOT), 0)
    c = lax.broadcasted_iota(jnp.int32, (SLOT, SLOT), 1)
    group_ones = jnp.where((r < A_HEAD_DIM) == (c < A_HEAD_DIM), 1.0, 0.0).astype(BF16)
    for z, (x_ref, o_ref) in enumerate(((kx_ref, kc_o), (vx_ref, vc_o))):
        xs[...] = x_ref[...].astype(F32)
        first = jnp.zeros((n, SLOT), F32)
        second = jnp.zeros((n, SLOT), F32)
        posc = jnp.zeros((SUBLANES, SLOT), F32)
        for j in range(CMP_STRIDE):
            xj = xs[:, j, :].astype(BF16)
            first = first + _dot(xj, w1_ref[z, j])
            second = second + _dot(xj, w1_ref[z, CMP_STRIDE + j])
        for j in range(CMP_LEN):
            posc = posc + _dot(pos_ref[z, j], w1_ref[z, j])
        hid = first + pltpu.roll(second, n - 1, 0) + posc[0:1, :]
        hid = hid * jax.nn.sigmoid(hid)
        comp = _dot(hid.astype(BF16), w2_ref[z])
        if z == 0:
            ms = _dot((comp * comp).astype(BF16), group_ones) * (1.0 / A_HEAD_DIM)
            comp = (comp * lax.rsqrt(ms + NORM_EPS)) * kg_ref[...]
        comp = comp.astype(BF16)
        for g in range(A_GROUPS):
            pick = jnp.where(jnp.logical_and(c < A_HEAD_DIM, r == c + g * A_HEAD_DIM), 1.0, 0.0).astype(BF16)
            o_ref[g] = _dot(comp, pick).astype(BF16)


def nsa_compress(pj4, w1, w2, pos, kg2):
    bsz, n, _, _ = pj4.shape

    def x_spec(off):
        return pl.BlockSpec((None, n, CMP_STRIDE, SLOT), lambda b: (b, 0, 0, off // SLOT))

    def const(shape):
        return pl.BlockSpec(shape, lambda b: (0,) * len(shape))

    o_spec = pl.BlockSpec((None, A_GROUPS, n, SLOT), lambda b: (b, 0, 0, 0))
    so = jax.ShapeDtypeStruct((bsz, A_GROUPS, n, SLOT), BF16)
    return pl.pallas_call(
        _compress_kernel,
        grid=(bsz,),
        in_specs=[x_spec(P_KC), x_spec(P_VC), const(w1.shape), const(w2.shape), const(pos.shape), const((1, LANE))],
        out_specs=[o_spec, o_spec],
        out_shape=[so, so],
        scratch_shapes=[pltpu.VMEM((n, CMP_STRIDE, SLOT), F32)],
        compiler_params=_cparams(("parallel",)),
        name="nsa_compress",
    )(pj4, pj4, w1, w2, pos, kg2)


def _cmp_attn_kernel(q_ref, kc_ref, vc_ref, ovl_ref, oc_o, sel_o, vct_s, *, tq):
    t0 = pl.program_id(2) * tq
    n_cmp = kc_ref.shape[0]
    n_sel = ovl_ref.shape[0]
    n = A_HPG * tq
    qst = jnp.concatenate([q_ref[:, h * SLOT:(h + 1) * SLOT] for h in range(A_HPG)], axis=0)
    blk = lax.broadcasted_iota(jnp.int32, (n_cmp, n), 0)
    t = t0 + (lax.broadcasted_iota(jnp.int32, (n_cmp, n), 1) & (tq - 1))
    s = jnp.where(t >= blk * CMP_STRIDE + (CMP_LEN - 1), _dot_nt(kc_ref[...], qst), NEG_INF)
    p = jnp.exp2(s - jnp.max(s, axis=0, keepdims=True))
    vct_s[...] = _value_t(vc_ref[...], A_HEAD_DIM)
    acc = _dot(vct_s[...], p.astype(BF16))
    inv_l = 1.0 / acc[A_HEAD_DIM:A_HEAD_DIM + 1, :]
    tq_row = t0 + (lax.broadcasted_iota(jnp.int32, (1, n), 1) & (tq - 1))
    inv_l = jnp.where(tq_row >= CMP_LEN - 1, inv_l, 0.0)
    out = acc * inv_l
    p = p * inv_l
    psum = p[:, 0:tq]
    for h in range(A_HPG):
        oc_o[:, h * SLOT:(h + 1) * SLOT] = out[:, h * tq:(h + 1) * tq].T.astype(oc_o.dtype)
        if h:
            psum = psum + p[:, h * tq:(h + 1) * tq]
    imp = jnp.dot(ovl_ref[...], psum, precision=lax.Precision.HIGHEST, preferred_element_type=F32)
    j = lax.broadcasted_iota(jnp.int32, (n_sel, tq), 0)
    cur = (t0 + lax.broadcasted_iota(jnp.int32, (n_sel, tq), 1)) // SEL_LEN
    big = -NEG_INF
    imp = jnp.where(j == 0, 3.0 * big, jnp.where(j == cur, 2.0 * big, jnp.where(j == cur - 1, big, imp)))
    imp = jnp.where(j > cur, NEG_INF, imp)
    work = imp
    kth = None
    for _ in range(min(SEL_TOPK, n_sel)):
        kth = jnp.max(work, axis=0, keepdims=True)
        work = jnp.where(work >= kth, -jnp.inf, work)
    notsel = jnp.where(jnp.logical_and(imp >= kth, j <= cur), 0.0, 1.0)
    free = SLOT - A_HEAD_DIM
    parts = [jnp.zeros((A_HEAD_DIM, tq), F32), notsel]
    if n_sel < free:
        parts.append(jnp.zeros((free - n_sel, tq), F32))
    sel_o[...] = jnp.concatenate(parts, axis=0).T.astype(sel_o.dtype)


def nsa_cmp_attn(qn, kc, vc, ovl_t, tq=512):
    bsz, s, _ = qn.shape
    n_cmp = kc.shape[2]
    n_sel = ovl_t.shape[0]
    gq = A_HPG * SLOT
    assert tq & (tq - 1) == 0
    kv_spec = pl.BlockSpec((None, None, n_cmp, SLOT), lambda b, g, i: (b, g, 0, 0))
    return pl.pallas_call(
        functools.partial(_cmp_attn_kernel, tq=tq),
        grid=(bsz, A_GROUPS, s // tq),
        in_specs=[pl.BlockSpec((None, tq, gq), lambda b, g, i: (b, i, g)),
                  kv_spec, kv_spec,
                  pl.BlockSpec(ovl_t.shape, lambda b, g, i: (0, 0))],
        out_specs=[pl.BlockSpec((None, tq, gq), lambda b, g, i: (b, i, g)),
                   pl.BlockSpec((None, None, tq, SLOT), lambda b, g, i: (b, g, i, 0))],
        out_shape=[jax.ShapeDtypeStruct((bsz, s, A_HEADS * SLOT), BF16),
                   jax.ShapeDtypeStruct((bsz, A_GROUPS, s, SLOT), BF16)],
        scratch_shapes=[pltpu.VMEM((SLOT, n_cmp), BF16)],
        compiler_params=_cparams(("parallel", "parallel", "parallel")),
        name="nsa_cmp_attn",
    )(qn, kc, vc, ovl_t)


def _softmax_tile(s, m_prev, acc_prev, vt):
    m_new = jnp.maximum(m_prev, jnp.max(s, axis=0, keepdims=True))
    alpha = jnp.exp2(m_prev - m_new)
    p = jnp.exp2(s - m_new).astype(BF16)
    return m_new, alpha * acc_prev + _dot(vt, p)


def _visible(kind, k0, q0, tk, tq, n):
    key = k0 + lax.broadcasted_iota(jnp.int32, (tk, n), 0)
    qry = q0 + (lax.broadcasted_iota(jnp.int32, (tk, n), 1) & (tq - 1))
    return key <= qry if kind == "causal" else key > qry - WINDOW


def _store_heads(o_ref, b, hpg, tq, acc, real_v):
    out = acc / acc[real_v:real_v + 1, :]
    for h in range(hpg):
        o_ref[:, (b * hpg + h) * SLOT:(b * hpg + h + 1) * SLOT] = out[:, h * tq:(h + 1) * tq].T.astype(o_ref.dtype)


def _flash_kernel(qi_ref, kj_ref, fl_ref, *refs, hb, hpg, real_v, select, tq, tk):
    if select:
        q_ref, k_ref, vt_ref, nsel_ref, o_ref, qst_s, m_s, acc_s = refs
    else:
        q_ref, k_ref, vt_ref, o_ref, qst_s, m_s, acc_s = refs
    step_id = pl.program_id(2)
    q0 = qi_ref[step_id] * tq
    k0 = kj_ref[step_id] * tk
    flags = fl_ref[step_id]
    n = hpg * tq

    @pl.when((flags & STEP_FIRST) != 0)
    def _():
        for b in range(hb):
            for h in range(hpg):
                qh = q_ref[:, (b * hpg + h) * SLOT:(b * hpg + h + 1) * SLOT]
                if select:
                    qh = qh + nsel_ref[b]
                qst_s[b, h * tq:(h + 1) * tq, :] = qh
        m_s[...] = jnp.full(m_s.shape, NEG_INF, F32)
        acc_s[...] = jnp.zeros(acc_s.shape, F32)

    def step(kind):
        visible = _visible(kind, k0, q0, tk, tq, n) if kind != "none" else None
        for b in range(hb):
            s = _dot_nt(k_ref[:, b * SLOT:(b + 1) * SLOT], qst_s[b])
            if visible is not None:
                s = jnp.where(visible, s, NEG_INF)
            m_s[b], acc_s[b] = _softmax_tile(s, m_s[b], acc_s[b], vt_ref[b * SLOT:(b + 1) * SLOT, :])

    @pl.when((flags & STEP_DIAG) == 0)
    def _():
        step("none")

    @pl.when((flags & STEP_DIAG) != 0)
    def _():
        step("causal")

    @pl.when((flags & STEP_LAST) != 0)
    def _():
        for b in range(hb):
            _store_heads(o_ref, b, hpg, tq, acc_s[b], real_v)


STEP_FIRST, STEP_LAST, STEP_DIAG = 1, 2, 4


def _flash_steps(n_q, tq, tk):
    qi, kj, fl = [], [], []
    for i in range(n_q):
        last = (i * tq + tq - 1) // tk
        for j in range(last + 1):
            diag = j * tk + tk - 1 > i * tq
            qi.append(i)
            kj.append(j)
            fl.append((STEP_FIRST if j == 0 else 0) | (STEP_LAST if j == last else 0) | (STEP_DIAG if diag else 0))
    return [jnp.asarray(np.asarray(a, np.int32)) for a in (qi, kj, fl)]


def flash_attention(q, k, vt, nsel=None, *, hb, hpg, real_v, tq, tk):
    bsz, s, qw = q.shape
    groups = qw // (hpg * SLOT)
    assert tq & (tq - 1) == 0 and groups % hb == 0
    tables = _flash_steps(s // tq, tq, tk)
    n_steps = tables[0].shape[0]
    in_specs = [pl.BlockSpec((None, tq, hb * hpg * SLOT), lambda b, g, t, qi, kj, fl: (b, qi[t], g)),
                pl.BlockSpec((None, tk, hb * SLOT), lambda b, g, t, qi, kj, fl: (b, kj[t], g)),
                pl.BlockSpec((None, hb * SLOT, tk), lambda b, g, t, qi, kj, fl: (b, g, kj[t]))]
    args = [q, k, vt]
    if nsel is not None:
        in_specs.append(pl.BlockSpec((None, hb, tq, SLOT), lambda b, g, t, qi, kj, fl: (b, g, qi[t], 0)))
        args.append(nsel)
    return pl.pallas_call(
        functools.partial(_flash_kernel, hb=hb, hpg=hpg, real_v=real_v, select=nsel is not None, tq=tq, tk=tk),
        grid_spec=pltpu.PrefetchScalarGridSpec(
            num_scalar_prefetch=3,
            grid=(bsz, groups // hb, n_steps),
            in_specs=in_specs,
            out_specs=pl.BlockSpec((None, tq, hb * hpg * SLOT), lambda b, g, t, qi, kj, fl: (b, qi[t], g)),
            scratch_shapes=[pltpu.VMEM((hb, hpg * tq, SLOT), BF16), pltpu.VMEM((hb, 1, hpg * tq), F32),
                            pltpu.VMEM((hb, SLOT, hpg * tq), F32)]),
        out_shape=jax.ShapeDtypeStruct((bsz, s, qw), BF16),
        compiler_params=_cparams(("parallel", "parallel", "arbitrary")),
        name="flash_select" if nsel is not None else "flash_causal",
    )(*tables, *args)


def _window_kernel(q_ref, *refs, hb, hpg, real_v, tq, nk):
    k_refs, vt_refs, o_ref = refs[:nk], refs[nk:2 * nk], refs[2 * nk]
    i = pl.program_id(2)
    q0 = i * tq
    n = hpg * tq
    for b in range(hb):
        qst = jnp.concatenate([q_ref[:, (b * hpg + h) * SLOT:(b * hpg + h + 1) * SLOT] for h in range(hpg)], axis=0)
        m = jnp.full((1, n), NEG_INF, F32)
        acc = jnp.zeros((SLOT, n), F32)
        for t in range(nk):
            back = nk - 1 - t
            s = _dot_nt(k_refs[t][:, b * SLOT:(b + 1) * SLOT], qst)
            if t == 0:
                s = jnp.where(jnp.logical_and(_visible("lower", q0 - back * tq, q0, tq, tq, n), i >= back), s, NEG_INF)
            elif t == nk - 1:
                s = jnp.where(_visible("causal", q0, q0, tq, tq, n), s, NEG_INF)
            else:
                s = jnp.where(i >= back, s, NEG_INF)
            m, acc = _softmax_tile(s, m, acc, vt_refs[t][b * SLOT:(b + 1) * SLOT, :])
        _store_heads(o_ref, b, hpg, tq, acc, real_v)


def window_attention(q, k, vt, *, hb, hpg, real_v, tq):
    bsz, s, qw = q.shape
    groups = qw // (hpg * SLOT)
    assert tq & (tq - 1) == 0 and groups % hb == 0 and WINDOW % tq == 0
    nk = WINDOW // tq + 1

    def k_spec(t):
        return pl.BlockSpec((None, tq, hb * SLOT), lambda b, g, i: (b, jnp.maximum(i - (nk - 1 - t), 0), g))

    def vt_spec(t):
        return pl.BlockSpec((None, hb * SLOT, tq), lambda b, g, i: (b, g, jnp.maximum(i - (nk - 1 - t), 0)))

    q_spec = pl.BlockSpec((None, tq, hb * hpg * SLOT), lambda b, g, i: (b, i, g))
    return pl.pallas_call(
        functools.partial(_window_kernel, hb=hb, hpg=hpg, real_v=real_v, tq=tq, nk=nk),
        grid=(bsz, groups // hb, s // tq),
        in_specs=[q_spec] + [k_spec(t) for t in range(nk)] + [vt_spec(t) for t in range(nk)],
        out_specs=q_spec,
        out_shape=jax.ShapeDtypeStruct((bsz, s, qw), BF16),
        compiler_params=_cparams(("parallel", "parallel", "parallel")),
        name="flash_window",
    )(q, *([k] * nk), *([vt] * nk))


def _mla_prep_kernel(ckv_ref, kpe_ref, cq_ref, cqg_ref, ckvg_ref, wuq_ref, wuk_ref, wuv_ref,
                     cos_ref, sin_ref, qg_ref, kg_ref, q_o, k_o, v_o):
    def rms(x, g):
        x = x.astype(F32)
        return ((x * lax.rsqrt(jnp.mean(x * x, axis=-1, keepdims=True) + NORM_EPS)) * g).astype(BF16)

    cq = rms(cq_ref[...], cqg_ref[...])
    ckv = rms(ckv_ref[...], ckvg_ref[...])
    q = _dot(cq, wuq_ref[...])
    kn = _dot(ckv, wuk_ref[...])
    vv = _dot(ckv, wuv_ref[...])
    kpe = kpe_ref[...].astype(F32)
    cos, sin = cos_ref[...], sin_ref[...]
    qg, kg = qg_ref[...], kg_ref[...]
    scale = B_QK ** -0.5
    rot = _rot_matrix(B_ROPE // 2, B_NOPE)
    for h in range(B_HEADS):
        sl = slice(h * SLOT, (h + 1) * SLOT)
        yq = _slot_rope(_slot_rms(q[:, sl], qg, B_QK), cos, sin, rot)
        q_o[:, sl] = (yq * (scale * LOG2E)).astype(BF16)
        yk = _slot_rope(_slot_rms(kn[:, sl] + kpe, kg, B_QK), cos, sin, rot)
        k_o[:, sl] = yk.astype(BF16)
        v_o[sl, :] = _value_t(vv[:, sl], B_V)


def mla_prep(pb, cqg, ckvg, wuq, wuk, wuv, tabs, qg, kg, tm=512):
    bsz, s, _ = pb.shape
    hw = B_HEADS * SLOT

    def const(shape):
        return pl.BlockSpec(shape, lambda b, i: (0,) * len(shape))

    tab = pl.BlockSpec((None, tm, LANE), lambda b, i: (b, i, 0))
    out = pl.BlockSpec((None, tm, hw), lambda b, i: (b, i, 0))
    so = jax.ShapeDtypeStruct((bsz, s, hw), BF16)
    return pl.pallas_call(
        _mla_prep_kernel,
        grid=(bsz, s // tm),
        in_specs=[pl.BlockSpec((None, tm, B_KV_LORA), lambda b, i: (b, i, P_CKV // B_KV_LORA)),
                  pl.BlockSpec((None, tm, SLOT), lambda b, i: (b, i, P_PE // SLOT)),
                  pl.BlockSpec((None, tm, B_Q_LORA), lambda b, i: (b, i, P_CQ // B_Q_LORA)),
                  const((1, B_Q_LORA)), const((1, B_KV_LORA)),
                  const(wuq.shape), const(wuk.shape), const(wuv.shape),
                  tab, tab, const((1, LANE)), const((1, LANE))],
        out_specs=[out, out, pl.BlockSpec((None, hw, tm), lambda b, i: (b, 0, i))],
        out_shape=[so, so, jax.ShapeDtypeStruct((bsz, hw, s), BF16)],
        compiler_params=_cparams(("parallel", "parallel")),
        name="mla_prep",
    )(pb, pb, pb, cqg, ckvg, wuq, wuk, wuv, *tabs, qg, kg)


SUBLANES = 8


def _rglru_kernel(cg_ref, cx_ref, cw_ref, cb_ref, wa_ref, ba_ref, wx_ref, bx_ref, lam_ref, o_ref,
                  xbuf, a_s, b_s, carry, *, ts):
    @pl.when(pl.program_id(1) == 0)
    def _():
        xbuf[0:SUBLANES, :] = jnp.zeros((SUBLANES, C_WIDTH), F32)
        carry[...] = jnp.zeros(carry.shape, F32)

    xr = cx_ref[...].astype(F32)
    xbuf[SUBLANES:SUBLANES + ts, :] = xr
    xc = cb_ref[...] + xr * cw_ref[C_CONV - 1:C_CONV, :]
    for w in range(C_CONV - 1):
        back = C_CONV - 1 - w
        xc = xc + xbuf[pl.ds(SUBLANES - back, ts), :] * cw_ref[w:w + 1, :]
    xbuf[0:SUBLANES, :] = xr[ts - SUBLANES:ts, :]

    for n in range(C_BLOCKS):
        sl = slice(n * C_BLOCK_W, (n + 1) * C_BLOCK_W)
        xb = xc[:, sl].astype(BF16)
        a_s[:, sl] = _dot(xb, wa_ref[n])
        b_s[:, sl] = _dot(xb, wx_ref[n])
    r = _sigmoid(a_s[...] + ba_ref[...])
    ig = _sigmoid(b_s[...] + bx_ref[...])
    nlam = -lam_ref[...]
    softplus = jnp.maximum(nlam, 0.0) + jnp.log1p(jnp.exp(-jnp.abs(nlam)))
    log_a = (-C_SCALE) * r * softplus
    a_s[...] = jnp.exp(log_a)
    th = jnp.tanh(log_a)
    b_s[...] = jnp.sqrt((-2.0 * th) / (1.0 - th)) * (ig * xc)

    row = lax.broadcasted_iota(jnp.int32, (SUBLANES, C_WIDTH), 0)

    def group(gi, h_prev):
        off = pl.multiple_of(gi * SUBLANES, SUBLANES)
        a = a_s[pl.ds(off, SUBLANES), :]
        b = b_s[pl.ds(off, SUBLANES), :]
        for d in (1, 2, 4):
            a_sh = pltpu.roll(a, d, 0)
            b_sh = pltpu.roll(b, d, 0)
            keep = row >= d
            b = jnp.where(keep, a * b_sh + b, b)
            a = jnp.where(keep, a * a_sh, a)
        h = b + a * h_prev
        b_s[pl.ds(off, SUBLANES), :] = h
        return h[SUBLANES - 1:SUBLANES, :]

    carry[...] = lax.fori_loop(0, ts // SUBLANES, group, carry[...])
    gate = cg_ref[...].astype(F32)
    gelu =0.5 * gate * (1.0 + jnp.tanh(0.7978845608028654 * (gate + 0.044715 * (gate * gate * gate))))
    o_ref[...] = (gelu * b_s[...]).astype(o_ref.dtype)


def rglru(pc, cw, cb, wa, ba, wx, bx, lam, ts=256):
    bsz, s, _ = pc.shape

    def const(shape):
        return pl.BlockSpec(shape, lambda b, i: (0,) * len(shape))

    vec = const((1, C_WIDTH))
    return pl.pallas_call(
        functools.partial(_rglru_kernel, ts=ts),
        grid=(bsz, s // ts),
        in_specs=[pl.BlockSpec((None, ts, C_WIDTH), lambda b, i: (b, i, P_CG // C_WIDTH)),
                  pl.BlockSpec((None, ts, C_WIDTH), lambda b, i: (b, i, P_CX // C_WIDTH)),
                  const((C_CONV, C_WIDTH)), vec, const(wa.shape), vec, const(wx.shape), vec, vec],
        out_specs=pl.BlockSpec((None, ts, C_WIDTH), lambda b, i: (b, i, 0)),
        out_shape=jax.ShapeDtypeStruct((bsz, s, C_WIDTH), BF16),
        scratch_shapes=[pltpu.VMEM((ts + SUBLANES, C_WIDTH), F32), pltpu.VMEM((ts, C_WIDTH), F32),
                        pltpu.VMEM((ts, C_WIDTH), F32), pltpu.VMEM((1, C_WIDTH), F32)],
        compiler_params=_cparams(("parallel", "arbitrary")),
        name="rglru",
    )(pc, pc, cw, cb, wa, ba, wx, bx, lam)


def _merge_kernel(oc_ref, os_ref, ow_ref, ag_ref, yb_ref, yc_ref, ma_ref, mb_ref, mc_ref, x_ref,
                  wpa_ref, wpb_ref, wpc_ref, wo_ref, o_ref, ya_s):
    gates = jax.nn.sigmoid(ag_ref[...].astype(F32))
    for h in range(A_HEADS):
        sl = slice(h * SLOT, (h + 1) * SLOT)
        y = (gates[:, 3 * h:3 * h + 1] * oc_ref[:, sl].astype(F32)
             + gates[:, 3 * h + 1:3 * h + 2] * os_ref[:, sl].astype(F32)
             + gates[:, 3 * h + 2:3 * h + 3] * ow_ref[:, sl].astype(F32))
        ya_s[:, sl] = y.astype(BF16)
    merged = (jax.nn.sigmoid(ma_ref[...].astype(F32)) * _dot(ya_s[...], wpa_ref[...])
              + jax.nn.sigmoid(mb_ref[...].astype(F32)) * _dot(yb_ref[...], wpb_ref[...])
              + jax.nn.sigmoid(mc_ref[...].astype(F32)) * _dot(yc_ref[...], wpc_ref[...]))
    o_ref[...] = x_ref[...] + _dot(merged.astype(BF16), wo_ref[...])


def merge(oc, osel, ow, pa2d, yb, yc, pm, x2d, wpa, wpb, wpc, wo, tm=512):
    t = x2d.shape[0]
    d = D_MODEL

    def rows(width, off=0):
        return pl.BlockSpec((tm, width), lambda i: (i, off // width))

    w_spec = pl.BlockSpec((d, d), lambda i: (0, 0), pipeline_mode=pl.Buffered(1))
    return pl.pallas_call(
        _merge_kernel,
        grid=(t // tm,),
        in_specs=[rows(d), rows(d), rows(d), rows(SLOT, P_AG), rows(d), rows(d),
                  rows(d, P_MA), rows(d, P_MB), rows(d, P_MC), rows(d),
                  w_spec, w_spec, w_spec, w_spec],
        out_specs=rows(d),
        out_shape=jax.ShapeDtypeStruct((t, d), F32),
        scratch_shapes=[pltpu.VMEM((tm, d), BF16)],
        compiler_params=_cparams(("parallel",)),
        name="merge",
    )(oc, osel, ow, pa2d, yb, yc, pm, pm, pm, x2d, wpa, wpb, wpc, wo)


def _ffn_kernel(x_ref, g_ref, w1_ref, w3_ref, w2_ref, o_ref, h_s, acc_s, *, nf):
    f = pl.program_id(1)

    @pl.when(f == 0)
    def _():
        x = x_ref[...]
        inv = lax.rsqrt(jnp.mean(x * x, axis=-1, keepdims=True) + NORM_EPS)
        h_s[...] = ((x * inv) * g_ref[...]).astype(BF16)
        acc_s[...] = jnp.zeros(acc_s.shape, F32)

    h = h_s[...]
    u = _dot(h, w1_ref[...])
    z = (u * jax.nn.sigmoid(u)) * _dot(h, w3_ref[...])
    acc_s[...] += _dot(z.astype(BF16), w2_ref[...])

    @pl.when(f == nf - 1)
    def _():
        o_ref[...] = x_ref[...] + acc_s[...]


def ffn(x2d, g, w1, w3, w2, tm=1024, tf=256):
    t, d = x2d.shape
    hidden = w1.shape[1]
    nf = hidden // tf
    return pl.pallas_call(
        functools.partial(_ffn_kernel, nf=nf),
        grid=(t // tm, nf),
        in_specs=[pl.BlockSpec((tm, d), lambda i, f: (i, 0)),
                  pl.BlockSpec((1, d), lambda i, f: (0, 0)),
                  pl.BlockSpec((d, tf), lambda i, f: (0, f)),
                  pl.BlockSpec((d, tf), lambda i, f: (0, f)),
                  pl.BlockSpec((tf, d), lambda i, f: (f, 0))],
        out_specs=pl.BlockSpec((tm, d), lambda i, f: (i, 0)),
        out_shape=jax.ShapeDtypeStruct((t, d), F32),
        scratch_shapes=[pltpu.VMEM((tm, d), BF16), pltpu.VMEM((tm, d), F32)],
        compiler_params=_cparams(("parallel", "arbitrary")),
        name="ffn",
    )(x2d, g.reshape(1, d), w1, w3, w2)


def _pad_cols(w, width):
    return jnp.pad(w, ((0, 0), (0, width - w.shape[1])))


def _slot_cols(w, heads, real):
    k = w.shape[0]
    return jnp.pad(w.reshape(k, heads, real), ((0, 0), (0, 0), (0, SLOT - real))).reshape(k, heads * SLOT)


def _slot_rows(w, heads, real):
    n = w.shape[1]
    return jnp.pad(w.reshape(heads, real, n), ((0, 0), (0, SLOT - real), (0, 0))).reshape(heads * SLOT, n)


def _split_w_in(w):
    sizes = (A_Q, A_KV, A_KV, A_KV, A_KV, A_KV, A_KV, A_GATES, B_Q_LORA, B_KV_LORA, B_ROPE,
             C_WIDTH, C_WIDTH, D_MODEL, D_MODEL, D_MODEL)
    offs = np.cumsum((0,) + sizes)
    (aq, akc, avc, aks, avs, akw, avw, ag, bcq, bckv, bkpe, cg, cx, ma, mb, mc) = [
        w[:, offs[n]:offs[n + 1]] for n in range(len(sizes))]
    kpe_slot = jnp.pad(bkpe, ((0, 0), (B_NOPE, SLOT - B_NOPE - B_ROPE)))
    pieces = [(P_AQ, _slot_cols(aq, A_HEADS, A_HEAD_DIM)), (P_CG, cg), (P_CX, cx), (P_MA, ma), (P_MB, mb),
              (P_MC, mc), (P_KS, _slot_cols(aks, A_GROUPS, A_HEAD_DIM)), (P_VS, _slot_cols(avs, A_GROUPS, A_HEAD_DIM)),
              (P_KW, _slot_cols(akw, A_GROUPS, A_HEAD_DIM)), (P_VW, _slot_cols(avw, A_GROUPS, A_HEAD_DIM)),
              (P_CKV, bckv), (P_KC, akc), (P_VC, avc), (P_AG, _pad_cols(ag, SLOT)), (P_PE, kpe_slot), (P_CQ, bcq)]
    cols, at = [], 0
    for off, piece in pieces:
        assert off >= at
        if off > at:
            cols.append(jnp.zeros((w.shape[0], off - at), w.dtype))
        cols.append(piece)
        at = off + piece.shape[1]
    assert at == P_N
    return jnp.concatenate(cols, axis=1).astype(BF16)


def _rope_angles(positions, half):
    inv = ROPE_THETA ** (-jnp.arange(half, dtype=F32) / half)
    ang = positions.astype(F32)[..., None] * inv
    return jnp.cos(ang), jnp.sin(ang)


def _rope_tables(cos, sin, lo):
    shape, half = cos.shape[:-1], cos.shape[-1]
    tail = SLOT - lo - 2 * half
    cos_t = jnp.concatenate([jnp.ones(shape + (lo,), F32), cos, cos, jnp.ones(shape + (tail,), F32)], axis=-1)
    sin_t = jnp.concatenate([jnp.zeros(shape + (lo,), F32), sin, sin, jnp.zeros(shape + (tail,), F32)], axis=-1)
    return cos_t, sin_t


def _overlap_t(s):
    n_cmp = s // CMP_STRIDE
    n_sel = s // SEL_LEN
    starts = np.arange(n_cmp) * CMP_STRIDE
    sel_starts = np.arange(n_sel) * SEL_LEN
    ovl = np.clip(np.minimum(starts[:, None] + CMP_LEN, sel_starts[None, :] + SEL_LEN)
                  - np.maximum(starts[:, None], sel_starts[None, :]), 0, None) / CMP_LEN
    ovl[(s - CMP_LEN) // CMP_STRIDE + 1:, :] = 0.0
    return jnp.asarray(ovl.T, dtype=F32)


def _both_groups(w):
    z = jnp.zeros_like(w)
    return jnp.concatenate([jnp.concatenate([w, z], axis=-1), jnp.concatenate([z, w], axis=-1)], axis=-2)


def _vec_slot(g):
    return jnp.pad(g, (0, SLOT - g.shape[0])).reshape(1, SLOT).astype(F32)


def kernel(x, positions, mix_norm_g, w_in, a_q_norm_g, a_k_norm_g, a_cmp_pos, a_cmp_w1, a_cmp_w2, b_cq_norm_g, b_ckv_norm_g, b_w_uq, b_w_ukv, b_q_norm_g, b_k_norm_g, c_conv_w, c_conv_b, c_w_a, c_b_a, c_w_x, c_b_x, c_lambda, w_pa, w_pb, w_pc, w_o, ffn_norm_g, ffn_w1, ffn_w3, ffn_w2):
    bsz, s, d = x.shape
    t = bsz * s
    depth = w_in.shape[0]
    n_chunk = s // CMP_STRIDE
    half_a, half_b = A_HEAD_DIM // 2, B_ROPE // 2
    cos, sin = _rope_angles(positions, half_a)
    tabs_a = _rope_tables(cos, sin, 0)
    assert half_a % half_b == 0
    step = half_a // half_b
    tabs_b = _rope_tables(cos[..., ::step], sin[..., ::step], B_NOPE)
    ovl_t = _overlap_t(s)

    x2d = x.reshape(t, d)
    for l in range(depth):
        pj = norm_matmul(x2d, mix_norm_g[l], _split_w_in(w_in[l]))
        pj3 = pj.reshape(bsz, s, P_N)

        qg, kg = _vec_slot(a_q_norm_g[l]), _vec_slot(a_k_norm_g[l])
        qn, qr, ks, vs, kw, vw = nsa_prep(pj3, tabs_a, qg, kg)

        w1 = _both_groups(a_cmp_w1[l].reshape(2, CMP_LEN, A_HEAD_DIM, A_HEAD_DIM)).astype(BF16)
        w2 = _both_groups(a_cmp_w2[l]).astype(BF16)
        pos = jnp.concatenate([a_cmp_pos[l]] * A_GROUPS, axis=-1)
        pos = jnp.broadcast_to(pos[:, :, None, :], (2, CMP_LEN, SUBLANES, SLOT)).astype(BF16)
        kg2 = jnp.concatenate([a_k_norm_g[l]] * A_GROUPS).reshape(1, SLOT).astype(F32)
        kc, vc = nsa_compress(pj3.reshape(bsz, n_chunk, CMP_STRIDE, P_N), w1, w2, pos, kg2)
        oc, nsel = nsa_cmp_attn(qn, kc, vc, ovl_t)
        osel = flash_attention(qr, ks, vs, nsel, hb=A_GROUPS, hpg=A_HPG, real_v=A_HEAD_DIM, tq=512, tk=512)
        ow = window_attention(qr, kw, vw, hb=A_GROUPS, hpg=A_HPG, real_v=A_HEAD_DIM, tq=256)

        wuq = _slot_cols(b_w_uq[l], B_HEADS, B_QK).astype(BF16)
        ukv = b_w_ukv[l].reshape(B_KV_LORA, B_HEADS, B_NOPE + B_V)
        wuk = _slot_cols(ukv[:, :, :B_NOPE].reshape(B_KV_LORA, B_HEADS * B_NOPE), B_HEADS, B_NOPE).astype(BF16)
        wuv = _slot_cols(ukv[:, :, B_NOPE:].reshape(B_KV_LORA, B_HEADS * B_V), B_HEADS, B_V).astype(BF16)
        bq, bk, bv = mla_prep(pj3, b_cq_norm_g[l].reshape(1, -1), b_ckv_norm_g[l].reshape(1, -1),
                              wuq, wuk, wuv, tabs_b, _vec_slot(b_q_norm_g[l]), _vec_slot(b_k_norm_g[l]))
        yb = flash_attention(bq, bk, bv, hb=2, hpg=1, real_v=B_V, tq=1024, tk=1024)

        yc = rglru(pj3, c_conv_w[l], c_conv_b[l].reshape(1, -1),
                   c_w_a[l].astype(BF16), c_b_a[l].reshape(1, -1), c_w_x[l].astype(BF16),
                   c_b_x[l].reshape(1, -1), c_lambda[l].reshape(1, -1))

        x2d = merge(oc.reshape(t, -1), osel.reshape(t, -1), ow.reshape(t, -1), pj,
                    yb.reshape(t, -1), yc.reshape(t, -1), pj, x2d,
                    _slot_rows(w_pa[l], A_HEADS, A_HEAD_DIM).astype(BF16),
                    _slot_rows(w_pb[l], B_HEADS, B_V).astype(BF16),
                    w_pc[l].astype(BF16), w_o[l].astype(BF16))
        x2d = ffn(x2d, ffn_norm_g[l], ffn_w1[l].astype(BF16), ffn_w3[l].astype(BF16), ffn_w2[l].astype(BF16))
    return x2d.reshape(bsz, s, d)
```

```python
import functools

import numpy as np
import jax
import jax.numpy as jnp
from jax import lax
from jax.experimental import pallas as pl
from jax.experimental.pallas import tpu as pltpu

F32 = jnp.float32
BF16 = jnp.bfloat16

D_MODEL = 1024
ROPE_THETA = 10000.0
NORM_EPS = 1e-6
NEG_INF = -1e30
LOG2E = 1.4426950408889634

A_HEADS = 8
A_GROUPS = 2
A_HPG = A_HEADS // A_GROUPS
A_HEAD_DIM = 64
CMP_LEN = 32
CMP_STRIDE = 16
SEL_LEN = 64
SEL_TOPK = 16
WINDOW = 512

B_HEADS = 8
B_Q_LORA = 384
B_KV_LORA = 256
B_NOPE = 64
B_ROPE = 32
B_V = 64
B_QK = B_NOPE + B_ROPE

C_WIDTH = 1024
C_BLOCKS = 8
C_BLOCK_W = C_WIDTH // C_BLOCKS
C_CONV = 4
C_SCALE = 8.0

FFN_HIDDEN = 2816

A_Q = A_HEADS * A_HEAD_DIM
A_KV = A_GROUPS * A_HEAD_DIM
A_GATES = 3 * A_HEADS

LANE = 128
SLOT = LANE
VMEM_LIMIT = 56 * 1024 * 1024

GW = A_GROUPS * SLOT
P_AQ = 0
P_CG = P_AQ + A_HEADS * SLOT
P_CX = P_CG + C_WIDTH
P_MA = P_CX + C_WIDTH
P_MB = P_MA + D_MODEL
P_MC = P_MB + D_MODEL
P_KS = P_MC + D_MODEL
P_VS = P_KS + GW
P_KW = P_VS + GW
P_VW = P_KW + GW
P_CKV = P_VW + GW
P_KC = P_CKV + B_KV_LORA
P_VC = P_KC + A_KV
P_AG = P_VC + A_KV
P_PE = P_AG + SLOT
P_CQ = 21 * B_Q_LORA
P_N = P_CQ + B_Q_LORA
assert P_PE + SLOT <= P_CQ


def _cparams(sem):
    return pltpu.CompilerParams(dimension_semantics=sem, vmem_limit_bytes=VMEM_LIMIT)


def _dot(a, b):
    return jnp.dot(a, b, preferred_element_type=F32)


def _sigmoid(x):
    return 0.5 * jnp.tanh(0.5 * x) + 0.5


def _dot_nt(a, b):
    return lax.dot_general(a, b, (((1,), (1,)), ((), ())), preferred_element_type=F32)


def _norm_matmul_kernel(x_ref, g_ref, w_ref, o_ref, h_ref):
    @pl.when(pl.program_id(1) == 0)
    def _():
        x = x_ref[...]
        inv = lax.rsqrt(jnp.mean(x * x, axis=-1, keepdims=True) + NORM_EPS)
        h_ref[...] = ((x * inv) * g_ref[...]).astype(BF16)

    o_ref[...] = _dot(h_ref[...], w_ref[...]).astype(o_ref.dtype)


def norm_matmul(x2d, g, w, tm=1024, tn=1408):
    t, k = x2d.shape
    n = w.shape[1]
    assert n % tn == 0 and tn % LANE == 0
    return pl.pallas_call(
        _norm_matmul_kernel,
        grid=(t // tm, n // tn),
        in_specs=[pl.BlockSpec((tm, k), lambda i, j: (i, 0)),
                  pl.BlockSpec((1, k), lambda i, j: (0, 0)),
                  pl.BlockSpec((k, tn), lambda i, j: (0, j))],
        out_specs=pl.BlockSpec((tm, tn), lambda i, j: (i, j)),
        out_shape=jax.ShapeDtypeStruct((t, n), BF16),
        scratch_shapes=[pltpu.VMEM((tm, k), BF16)],
        compiler_params=_cparams(("parallel", "arbitrary")),
        name="norm_matmul",
    )(x2d, g.reshape(1, k), w)


def _dot_split(x, m):
    hi = x.astype(BF16)
    lo = (x - hi.astype(F32)).astype(BF16)
    return _dot(hi, m) + _dot(lo, m)


def _slot_rms(x, g, width):
    x = x.astype(F32)
    ms = _dot((x * x).astype(BF16), jnp.ones((SLOT, SLOT), BF16)) * (1.0 / width)
    return (x * lax.rsqrt(ms + NORM_EPS)) * g


def _rot_matrix(half, lo):
    r = lax.broadcasted_iota(jnp.int32, (SLOT, SLOT), 0)
    c = lax.broadcasted_iota(jnp.int32, (SLOT, SLOT), 1)
    minus = jnp.logical_and(jnp.logical_and(c >= lo, c < lo + half), r == c + half)
    plus = jnp.logical_and(jnp.logical_and(c >= lo + half, c < lo + 2 * half), r == c - half)
    return jnp.where(minus, -1.0, jnp.where(plus, 1.0, 0.0)).astype(BF16)


def _slot_rope(y, cos, sin, rot):
    return y * cos + _dot_split(y, rot) * sin


def _value_t(v, real):
    vt = v.astype(F32).T
    row = lax.broadcasted_iota(jnp.int32, vt.shape, 0)
    return jnp.where(row == real, 1.0, vt).astype(BF16)


def _nsa_prep_kernel(aq_ref, ks_ref, vs_ref, kw_ref, vw_ref, cos_ref, sin_ref, qg_ref, kg_ref,
                     qn_o, qr_o, ks_o, vs_o, kw_o, vw_o, *, tm):
    cos, sin = cos_ref[...], sin_ref[...]
    qg, kg = qg_ref[...], kg_ref[...]
    scale = A_HEAD_DIM ** -0.5 * LOG2E
    rot = _rot_matrix(A_HEAD_DIM // 2, 0)
    for h in range(A_HEADS):
        sl = slice(h * SLOT, (h + 1) * SLOT)
        y = _slot_rms(aq_ref[:, sl], qg, A_HEAD_DIM)
        qn_o[:, sl] = (y * scale).astype(BF16)
        qr_o[:, sl] = (_slot_rope(y, cos, sin, rot) * scale).astype(BF16)
    t = pl.program_id(1) * tm + lax.broadcasted_iota(jnp.int32, (tm, SLOT), 0)
    lane = lax.broadcasted_iota(jnp.int32, (tm, SLOT), 1)
    blk_bias = jnp.where(lane == A_HEAD_DIM + t // SEL_LEN, NEG_INF, 0.0)
    for g in range(A_GROUPS):
        sl = slice(g * SLOT, (g + 1) * SLOT)
        ksel = _slot_rope(_slot_rms(ks_ref[:, sl], kg, A_HEAD_DIM), cos, sin, rot)
        ks_o[:, sl] = (ksel + blk_bias).astype(BF16)
        kw_o[:, sl] = _slot_rope(_slot_rms(kw_ref[:, sl], kg, A_HEAD_DIM), cos, sin, rot).astype(BF16)
        vs_o[sl, :] = _value_t(vs_ref[:, sl], A_HEAD_DIM)
        vw_o[sl, :] = _value_t(vw_ref[:, sl], A_HEAD_DIM)


def nsa_prep(pa, tabs, qg, kg, tm=512):
    bsz, s, _ = pa.shape
    gw = A_GROUPS * SLOT
    assert s // SEL_LEN <= SLOT - A_HEAD_DIM, "selection blocks must fit the free lanes of a head slot"

    def col(width, off):
        return pl.BlockSpec((None, tm, width), lambda b, i: (b, i, off // width))

    tab = pl.BlockSpec((None, tm, LANE), lambda b, i: (b, i, 0))
    vec = pl.BlockSpec((1, LANE), lambda b, i: (0, 0))
    out_q = pl.BlockSpec((None, tm, A_HEADS * SLOT), lambda b, i: (b, i, 0))
    out_k = pl.BlockSpec((None, tm, gw), lambda b, i: (b, i, 0))
    out_vt = pl.BlockSpec((None, gw, tm), lambda b, i: (b, 0, i))
    sq = jax.ShapeDtypeStruct((bsz, s, A_HEADS * SLOT), BF16)
    sk = jax.ShapeDtypeStruct((bsz, s, gw), BF16)
    svt = jax.ShapeDtypeStruct((bsz, gw, s), BF16)
    return pl.pallas_call(
        functools.partial(_nsa_prep_kernel, tm=tm),
        grid=(bsz, s // tm),
        in_specs=[col(A_HEADS * SLOT, P_AQ), col(gw, P_KS), col(gw, P_VS), col(gw, P_KW), col(gw, P_VW),
                  tab, tab, vec, vec],
        out_specs=[out_q, out_q, out_k, out_vt, out_k, out_vt],
        out_shape=[sq, sq, sk, svt, sk, svt],
        compiler_params=_cparams(("parallel", "parallel")),
        name="nsa_prep",
    )(pa, pa, pa, pa, pa, *tabs, qg, kg)


def _compress_kernel(kx_ref, vx_ref, w1_ref, w2_ref, pos_ref, kg_ref, kc_o, vc_o, xs):
    n = kx_ref.shape[0]
    r = lax.broadcasted_iota(jnp.int32, (SLOT, SLOT), 0)
    c = lax.broadcasted_iota(jnp.int32, (SLOT, SLOT), 1)
    group_ones = jnp.where((r < A_HEAD_DIM) == (c < A_HEAD_DIM), 1.0, 0.0).astype(BF16)
    for z, (x_ref, o_ref) in enumerate(((kx_ref, kc_o), (vx_ref, vc_o))):
        xs[...] = x_ref[...].astype(F32)
        first = jnp.zeros((n, SLOT), F32)
        second = jnp.zeros((n, SLOT), F32)
        posc = jnp.zeros((SUBLANES, SLOT), F32)
        for j in range(CMP_STRIDE):
            xj = xs[:, j, :].astype(BF16)
            first = first + _dot(xj, w1_ref[z, j])
            second = second + _dot(xj, w1_ref[z, CMP_STRIDE + j])
        for j in range(CMP_LEN):
            posc = posc + _dot(pos_ref[z, j], w1_ref[z, j])
        hid = first + pltpu.roll(second, n - 1, 0) + posc[0:1, :]
        hid = hid * jax.nn.sigmoid(hid)
        comp = _dot(hid.astype(BF16), w2_ref[z])
        if z == 0:
            ms = _dot((comp * comp).astype(BF16), group_ones) * (1.0 / A_HEAD_DIM)
            comp = (comp * lax.rsqrt(ms + NORM_EPS)) * kg_ref[...]
        comp = comp.astype(BF16)
        for g in range(A_GROUPS):
            pick = jnp.where(jnp.logical_and(c < A_HEAD_DIM, r == c + g * A_HEAD_DIM), 1.0, 0.0).astype(BF16)
            o_ref[g] = _dot(comp, pick).astype(BF16)


def nsa_compress(pj4, w1, w2, pos, kg2):
    bsz, n, _, _ = pj4.shape

    def x_spec(off):
        return pl.BlockSpec((None, n, CMP_STRIDE, SLOT), lambda b: (b, 0, 0, off // SLOT))

    def const(shape):
        return pl.BlockSpec(shape, lambda b: (0,) * len(shape))

    o_spec = pl.BlockSpec((None, A_GROUPS, n, SLOT), lambda b: (b, 0, 0, 0))
    so = jax.ShapeDtypeStruct((bsz, A_GROUPS, n, SLOT), BF16)
    return pl.pallas_call(
        _compress_kernel,
        grid=(bsz,),
        in_specs=[x_spec(P_KC), x_spec(P_VC), const(w1.shape), const(w2.shape), const(pos.shape), const((1, LANE))],
        out_specs=[o_spec, o_spec],
        out_shape=[so, so],
        scratch_shapes=[pltpu.VMEM((n, CMP_STRIDE, SLOT), F32)],
        compiler_params=_cparams(("parallel",)),
        name="nsa_compress",
    )(pj4, pj4, w1, w2, pos, kg2)


def _cmp_attn_kernel(q_ref, kc_ref, vc_ref, ovl_ref, oc_o, sel_o, vct_s, *, tq):
    t0 = pl.program_id(2) * tq
    n_cmp = kc_ref.shape[0]
    n_sel = ovl_ref.shape[0]
    n = A_HPG * tq
    qst = jnp.concatenate([q_ref[:, h * SLOT:(h + 1) * SLOT] for h in range(A_HPG)], axis=0)
    blk = lax.broadcasted_iota(jnp.int32, (n_cmp, n), 0)
    t = t0 + (lax.broadcasted_iota(jnp.int32, (n_cmp, n), 1) & (tq - 1))
    s = jnp.where(t >= blk * CMP_STRIDE + (CMP_LEN - 1), _dot_nt(kc_ref[...], qst), NEG_INF)
    p = jnp.exp2(s - jnp.max(s, axis=0, keepdims=True))
    vct_s[...] = _value_t(vc_ref[...], A_HEAD_DIM)
    acc = _dot(vct_s[...], p.astype(BF16))
    inv_l = 1.0 / acc[A_HEAD_DIM:A_HEAD_DIM + 1, :]
    tq_row = t0 + (lax.broadcasted_iota(jnp.int32, (1, n), 1) & (tq - 1))
    inv_l = jnp.where(tq_row >= CMP_LEN - 1, inv_l, 0.0)
    out = acc * inv_l
    p = p * inv_l
    psum = p[:, 0:tq]
    for h in range(A_HPG):
        oc_o[:, h * SLOT:(h + 1) * SLOT] = out[:, h * tq:(h + 1) * tq].T.astype(oc_o.dtype)
        if h:
            psum = psum + p[:, h * tq:(h + 1) * tq]
    imp = jnp.dot(ovl_ref[...], psum, precision=lax.Precision.HIGHEST, preferred_element_type=F32)
    j = lax.broadcasted_iota(jnp.int32, (n_sel, tq), 0)
    cur = (t0 + lax.broadcasted_iota(jnp.int32, (n_sel, tq), 1)) // SEL_LEN
    big = -NEG_INF
    imp = jnp.where(j == 0, 3.0 * big, jnp.where(j == cur, 2.0 * big, jnp.where(j == cur - 1, big, imp)))
    imp = jnp.where(j > cur, NEG_INF, imp)
    work = imp
    kth = None
    for _ in range(min(SEL_TOPK, n_sel)):
        kth = jnp.max(work, axis=0, keepdims=True)
        work = jnp.where(work >= kth, -jnp.inf, work)
    notsel = jnp.where(jnp.logical_and(imp >= kth, j <= cur), 0.0, 1.0)
    free = SLOT - A_HEAD_DIM
    parts = [jnp.zeros((A_HEAD_DIM, tq), F32), notsel]
    if n_sel < free:
        parts.append(jnp.zeros((free - n_sel, tq), F32))
    sel_o[...] = jnp.concatenate(parts, axis=0).T.astype(sel_o.dtype)


def nsa_cmp_attn(qn, kc, vc, ovl_t, tq=512):
    bsz, s, _ = qn.shape
    n_cmp = kc.shape[2]
    n_sel = ovl_t.shape[0]
    gq = A_HPG * SLOT
    assert tq & (tq - 1) == 0
    kv_spec = pl.BlockSpec((None, None, n_cmp, SLOT), lambda b, g, i: (b, g, 0, 0))
    return pl.pallas_call(
        functools.partial(_cmp_attn_kernel, tq=tq),
        grid=(bsz, A_GROUPS, s // tq),
        in_specs=[pl.BlockSpec((None, tq, gq), lambda b, g, i: (b, i, g)),
                  kv_spec, kv_spec,
                  pl.BlockSpec(ovl_t.shape, lambda b, g, i: (0, 0))],
        out_specs=[pl.BlockSpec((None, tq, gq), lambda b, g, i: (b, i, g)),
                   pl.BlockSpec((None, None, tq, SLOT), lambda b, g, i: (b, g, i, 0))],
        out_shape=[jax.ShapeDtypeStruct((bsz, s, A_HEADS * SLOT), BF16),
                   jax.ShapeDtypeStruct((bsz, A_GROUPS, s, SLOT), BF16)],
        scratch_shapes=[pltpu.VMEM((SLOT, n_cmp), BF16)],
        compiler_params=_cparams(("parallel", "parallel", "parallel")),
        name="nsa_cmp_attn",
    )(qn, kc, vc, ovl_t)


def _softmax_tile(s, m_prev, acc_prev, vt):
    m_new = jnp.maximum(m_prev, jnp.max(s, axis=0, keepdims=True))
    alpha = jnp.exp2(m_prev - m_new)
    p = jnp.exp2(s - m_new).astype(BF16)
    return m_new, alpha * acc_prev + _dot(vt, p)


def _visible(kind, k0, q0, tk, tq, n):
    key = k0 + lax.broadcasted_iota(jnp.int32, (tk, n), 0)
    qry = q0 + (lax.broadcasted_iota(jnp.int32, (tk, n), 1) & (tq - 1))
    return key <= qry if kind == "causal" else key > qry - WINDOW


def _store_heads(o_ref, b, hpg, tq, acc, real_v):
    out = acc / acc[real_v:real_v + 1, :]
    for h in range(hpg):
        o_ref[:, (b * hpg + h) * SLOT:(b * hpg + h + 1) * SLOT] = out[:, h * tq:(h + 1) * tq].T.astype(o_ref.dtype)


def _flash_kernel(qi_ref, kj_ref, fl_ref, *refs, hb, hpg, real_v, select, tq, tk):
    if select:
        q_ref, k_ref, vt_ref, nsel_ref, o_ref, qst_s, m_s, acc_s = refs
    else:
        q_ref, k_ref, vt_ref, o_ref, qst_s, m_s, acc_s = refs
    step_id = pl.program_id(2)
    q0 = qi_ref[step_id] * tq
    k0 = kj_ref[step_id] * tk
    flags = fl_ref[step_id]
    n = hpg * tq

    @pl.when((flags & STEP_FIRST) != 0)
    def _():
        for b in range(hb):
            for h in range(hpg):
                qh = q_ref[:, (b * hpg + h) * SLOT:(b * hpg + h + 1) * SLOT]
                if select:
                    qh = qh + nsel_ref[b]
                qst_s[b, h * tq:(h + 1) * tq, :] = qh
        m_s[...] = jnp.full(m_s.shape, NEG_INF, F32)
        acc_s[...] = jnp.zeros(acc_s.shape, F32)

    def step(kind):
        visible = _visible(kind, k0, q0, tk, tq, n) if kind != "none" else None
        for b in range(hb):
            s = _dot_nt(k_ref[:, b * SLOT:(b + 1) * SLOT], qst_s[b])
            if visible is not None:
                s = jnp.where(visible, s, NEG_INF)
            m_s[b], acc_s[b] = _softmax_tile(s, m_s[b], acc_s[b], vt_ref[b * SLOT:(b + 1) * SLOT, :])

    @pl.when((flags & STEP_DIAG) == 0)
    def _():
        step("none")

    @pl.when((flags & STEP_DIAG) != 0)
    def _():
        step("causal")

    @pl.when((flags & STEP_LAST) != 0)
    def _():
        for b in range(hb):
            _store_heads(o_ref, b, hpg, tq, acc_s[b], real_v)


STEP_FIRST, STEP_LAST, STEP_DIAG = 1, 2, 4


def _flash_steps(n_q, tq, tk):
    qi, kj, fl = [], [], []
    for i in range(n_q):
        last = (i * tq + tq - 1) // tk
        for j in range(last + 1):
            diag = j * tk + tk - 1 > i * tq
            qi.append(i)
            kj.append(j)
            fl.append((STEP_FIRST if j == 0 else 0) | (STEP_LAST if j == last else 0) | (STEP_DIAG if diag else 0))
    return [jnp.asarray(np.asarray(a, np.int32)) for a in (qi, kj, fl)]


def flash_attention(q, k, vt, nsel=None, *, hb, hpg, real_v, tq, tk):
    bsz, s, qw = q.shape
    groups = qw // (hpg * SLOT)
    assert tq & (tq - 1) == 0 and groups % hb == 0
    tables = _flash_steps(s // tq, tq, tk)
    n_steps = tables[0].shape[0]
    in_specs = [pl.BlockSpec((None, tq, hb * hpg * SLOT), lambda b, g, t, qi, kj, fl: (b, qi[t], g)),
                pl.BlockSpec((None, tk, hb * SLOT), lambda b, g, t, qi, kj, fl: (b, kj[t], g)),
                pl.BlockSpec((None, hb * SLOT, tk), lambda b, g, t, qi, kj, fl: (b, g, kj[t]))]
    args = [q, k, vt]
    if nsel is not None:
        in_specs.append(pl.BlockSpec((None, hb, tq, SLOT), lambda b, g, t, qi, kj, fl: (b, g, qi[t], 0)))
        args.append(nsel)
    return pl.pallas_call(
        functools.partial(_flash_kernel, hb=hb, hpg=hpg, real_v=real_v, select=nsel is not None, tq=tq, tk=tk),
        grid_spec=pltpu.PrefetchScalarGridSpec(
            num_scalar_prefetch=3,
            grid=(bsz, groups // hb, n_steps),
            in_specs=in_specs,
            out_specs=pl.BlockSpec((None, tq, hb * hpg * SLOT), lambda b, g, t, qi, kj, fl: (b, qi[t], g)),
            scratch_shapes=[pltpu.VMEM((hb, hpg * tq, SLOT), BF16), pltpu.VMEM((hb, 1, hpg * tq), F32),
                            pltpu.VMEM((hb, SLOT, hpg * tq), F32)]),
        out_shape=jax.ShapeDtypeStruct((bsz, s, qw), BF16),
        compiler_params=_cparams(("parallel", "parallel", "arbitrary")),
        name="flash_select" if nsel is not None else "flash_causal",
    )(*tables, *args)


def _window_kernel(q_ref, *refs, hb, hpg, real_v, tq, nk):
    k_refs, vt_refs, o_ref = refs[:nk], refs[nk:2 * nk], refs[2 * nk]
    i = pl.program_id(2)
    q0 = i * tq
    n = hpg * tq
    for b in range(hb):
        qst = jnp.concatenate([q_ref[:, (b * hpg + h) * SLOT:(b * hpg + h + 1) * SLOT] for h in range(hpg)], axis=0)
        m = jnp.full((1, n), NEG_INF, F32)
        acc = jnp.zeros((SLOT, n), F32)
        for t in range(nk):
            back = nk - 1 - t
            s = _dot_nt(k_refs[t][:, b * SLOT:(b + 1) * SLOT], qst)
            if t == 0:
                s = jnp.where(jnp.logical_and(_visible("lower", q0 - back * tq, q0, tq, tq, n), i >= back), s, NEG_INF)
            elif t == nk - 1:
                s = jnp.where(_visible("causal", q0, q0, tq, tq, n), s, NEG_INF)
            else:
                s = jnp.where(i >= back, s, NEG_INF)
            m, acc = _softmax_tile(s, m, acc, vt_refs[t][b * SLOT:(b + 1) * SLOT, :])
        _store_heads(o_ref, b, hpg, tq, acc, real_v)


def window_attention(q, k, vt, *, hb, hpg, real_v, tq):
    bsz, s, qw = q.shape
    groups = qw // (hpg * SLOT)
    assert tq & (tq - 1) == 0 and groups % hb == 0 and WINDOW % tq == 0
    nk = WINDOW // tq + 1

    def k_spec(t):
        return pl.BlockSpec((None, tq, hb * SLOT), lambda b, g, i: (b, jnp.maximum(i - (nk - 1 - t), 0), g))

    def vt_spec(t):
        return pl.BlockSpec((None, hb * SLOT, tq), lambda b, g, i: (b, g, jnp.maximum(i - (nk - 1 - t), 0)))

    q_spec = pl.BlockSpec((None, tq, hb * hpg * SLOT), lambda b, g, i: (b, i, g))
    return pl.pallas_call(
        functools.partial(_window_kernel, hb=hb, hpg=hpg, real_v=real_v, tq=tq, nk=nk),
        grid=(bsz, groups // hb, s // tq),
        in_specs=[q_spec] + [k_spec(t) for t in range(nk)] + [vt_spec(t) for t in range(nk)],
        out_specs=q_spec,
        out_shape=jax.ShapeDtypeStruct((bsz, s, qw), BF16),
        compiler_params=_cparams(("parallel", "parallel", "parallel")),
        name="flash_window",
    )(q, *([k] * nk), *([vt] * nk))


def _mla_prep_kernel(ckv_ref, kpe_ref, cq_ref, cqg_ref, ckvg_ref, wuq_ref, wuk_ref, wuv_ref,
                     cos_ref, sin_ref, qg_ref, kg_ref, q_o, k_o, v_o):
    def rms(x, g):
        x = x.astype(F32)
        return ((x * lax.rsqrt(jnp.mean(x * x, axis=-1, keepdims=True) + NORM_EPS)) * g).astype(BF16)

    cq = rms(cq_ref[...], cqg_ref[...])
    ckv = rms(ckv_ref[...], ckvg_ref[...])
    q = _dot(cq, wuq_ref[...])
    kn = _dot(ckv, wuk_ref[...])
    vv = _dot(ckv, wuv_ref[...])
    kpe = kpe_ref[...].astype(F32)
    cos, sin = cos_ref[...], sin_ref[...]
    qg, kg = qg_ref[...], kg_ref[...]
    scale = B_QK ** -0.5
    rot = _rot_matrix(B_ROPE // 2, B_NOPE)
    for h in range(B_HEADS):
        sl = slice(h * SLOT, (h + 1) * SLOT)
        yq = _slot_rope(_slot_rms(q[:, sl], qg, B_QK), cos, sin, rot)
        q_o[:, sl] = (yq * (scale * LOG2E)).astype(BF16)
        yk = _slot_rope(_slot_rms(kn[:, sl] + kpe, kg, B_QK), cos, sin, rot)
        k_o[:, sl] = yk.astype(BF16)
        v_o[sl, :] = _value_t(vv[:, sl], B_V)


def mla_prep(pb, cqg, ckvg, wuq, wuk, wuv, tabs, qg, kg, tm=512):
    bsz, s, _ = pb.shape
    hw = B_HEADS * SLOT

    def const(shape):
        return pl.BlockSpec(shape, lambda b, i: (0,) * len(shape))

    tab = pl.BlockSpec((None, tm, LANE), lambda b, i: (b, i, 0))
    out = pl.BlockSpec((None, tm, hw), lambda b, i: (b, i, 0))
    so = jax.ShapeDtypeStruct((bsz, s, hw), BF16)
    return pl.pallas_call(
        _mla_prep_kernel,
        grid=(bsz, s // tm),
        in_specs=[pl.BlockSpec((None, tm, B_KV_LORA), lambda b, i: (b, i, P_CKV // B_KV_LORA)),
                  pl.BlockSpec((None, tm, SLOT), lambda b, i: (b, i, P_PE // SLOT)),
                  pl.BlockSpec((None, tm, B_Q_LORA), lambda b, i: (b, i, P_CQ // B_Q_LORA)),
                  const((1, B_Q_LORA)), const((1, B_KV_LORA)),
                  const(wuq.shape), const(wuk.shape), const(wuv.shape),
                  tab, tab, const((1, LANE)), const((1, LANE))],
        out_specs=[out, out, pl.BlockSpec((None, hw, tm), lambda b, i: (b, 0, i))],
        out_shape=[so, so, jax.ShapeDtypeStruct((bsz, hw, s), BF16)],
        compiler_params=_cparams(("parallel", "parallel")),
        name="mla_prep",
    )(pb, pb, pb, cqg, ckvg, wuq, wuk, wuv, *tabs, qg, kg)


SUBLANES = 8


def _rglru_kernel(cg_ref, cx_ref, cw_ref, cb_ref, wa_ref, ba_ref, wx_ref, bx_ref, lam_ref, o_ref,
                  xbuf, a_s, b_s, carry, *, ts):
    @pl.when(pl.program_id(1) == 0)
    def _():
        xbuf[0:SUBLANES, :] = jnp.zeros((SUBLANES, C_WIDTH), F32)
        carry[...] = jnp.zeros(carry.shape, F32)

    xr = cx_ref[...].astype(F32)
    xbuf[SUBLANES:SUBLANES + ts, :] = xr
    xc = cb_ref[...] + xr * cw_ref[C_CONV - 1:C_CONV, :]
    for w in range(C_CONV - 1):
        back = C_CONV - 1 - w
        xc = xc + xbuf[pl.ds(SUBLANES - back, ts), :] * cw_ref[w:w + 1, :]
    xbuf[0:SUBLANES, :] = xr[ts - SUBLANES:ts, :]

    for n in range(C_BLOCKS):
        sl = slice(n * C_BLOCK_W, (n + 1) * C_BLOCK_W)
        xb = xc[:, sl].astype(BF16)
        a_s[:, sl] = _dot(xb, wa_ref[n])
        b_s[:, sl] = _dot(xb, wx_ref[n])
    r = _sigmoid(a_s[...] + ba_ref[...])
    ig = _sigmoid(b_s[...] + bx_ref[...])
    nlam = -lam_ref[...]
    softplus = jnp.maximum(nlam, 0.0) + jnp.log1p(jnp.exp(-jnp.abs(nlam)))
    log_a = r * ((-C_SCALE) * softplus)
    a_s[...] = jnp.exp(log_a)
    th = jnp.tanh(log_a)
    b_s[...] = jnp.sqrt((-2.0 * th) / (1.0 - th)) * (ig * xc)

    row = lax.broadcasted_iota(jnp.int32, (SUBLANES, C_WIDTH), 0)

    def group(gi, h_prev):
        off = pl.multiple_of(gi * SUBLANES, SUBLANES)
        a = a_s[pl.ds(off, SUBLANES), :]
        b = b_s[pl.ds(off, SUBLANES), :]
        for d in (1, 2, 4):
            a_sh = pltpu.roll(a, d, 0)
            b_sh = pltpu.roll(b, d, 0)
            keep = row >= d
            b = jnp.where(keep, a * b_sh + b, b)
            a = jnp.where(keep, a * a_sh, a)
        h = b + a * h_prev
        b_s[pl.ds(off, SUBLANES), :] = h
        return h[SUBLANES - 1:SUBLANES, :]

    carry[...] = lax.fori_loop(0, ts // SUBLANES, group, carry[...])
    gate = cg_ref[...].astype(F32)
    half_gate = 0.5 * gate
    gelu = half_gate + half_gate * jnp.tanh(gate * (0.7978845608028654 + 0.035677408136300125 * (gate * gate)))
    o_ref[...] = (gelu * b_s[...]).astype(o_ref.dtype)


def rglru(pc, cw, cb, wa, ba, wx, bx, lam, ts=256):
    bsz, s, _ = pc.shape

    def const(shape):
        return pl.BlockSpec(shape, lambda b, i: (0,) * len(shape))

    vec = const((1, C_WIDTH))
    return pl.pallas_call(
        functools.partial(_rglru_kernel, ts=ts),
        grid=(bsz, s // ts),
        in_specs=[pl.BlockSpec((None, ts, C_WIDTH), lambda b, i: (b, i, P_CG // C_WIDTH)),
                  pl.BlockSpec((None, ts, C_WIDTH), lambda b, i: (b, i, P_CX // C_WIDTH)),
                  const((C_CONV, C_WIDTH)), vec, const(wa.shape), vec, const(wx.shape), vec, vec],
        out_specs=pl.BlockSpec((None, ts, C_WIDTH), lambda b, i: (b, i, 0)),
        out_shape=jax.ShapeDtypeStruct((bsz, s, C_WIDTH), BF16),
        scratch_shapes=[pltpu.VMEM((ts + SUBLANES, C_WIDTH), F32), pltpu.VMEM((ts, C_WIDTH), F32),
                        pltpu.VMEM((ts, C_WIDTH), F32), pltpu.VMEM((1, C_WIDTH), F32)],
        compiler_params=_cparams(("parallel", "arbitrary")),
        name="rglru",
    )(pc, pc, cw, cb, wa, ba, wx, bx, lam)


def _merge_kernel(oc_ref, os_ref, ow_ref, ag_ref, yb_ref, yc_ref, ma_ref, mb_ref, mc_ref, x_ref,
                  wpa_ref, wpb_ref, wpc_ref, wo_ref, o_ref, ya_s):
    gates = jax.nn.sigmoid(ag_ref[...].astype(F32))
    for h in range(A_HEADS):
        sl = slice(h * SLOT, (h + 1) * SLOT)
        y = (gates[:, 3 * h:3 * h + 1] * oc_ref[:, sl].astype(F32)
             + gates[:, 3 * h + 1:3 * h + 2] * os_ref[:, sl].astype(F32)
             + gates[:, 3 * h + 2:3 * h + 3] * ow_ref[:, sl].astype(F32))
        ya_s[:, sl] = y.astype(BF16)
    merged = (jax.nn.sigmoid(ma_ref[...].astype(F32)) * _dot(ya_s[...], wpa_ref[...])
              + jax.nn.sigmoid(mb_ref[...].astype(F32)) * _dot(yb_ref[...], wpb_ref[...])
              + jax.nn.sigmoid(mc_ref[...].astype(F32)) * _dot(yc_ref[...], wpc_ref[...]))
    o_ref[...] = x_ref[...] + _dot(merged.astype(BF16), wo_ref[...])


def merge(oc, osel, ow, pa2d, yb, yc, pm, x2d, wpa, wpb, wpc, wo, tm=512):
    t = x2d.shape[0]
    d = D_MODEL

    def rows(width, off=0):
        return pl.BlockSpec((tm, width), lambda i: (i, off // width))

    w_spec = pl.BlockSpec((d, d), lambda i: (0, 0), pipeline_mode=pl.Buffered(1))
    return pl.pallas_call(
        _merge_kernel,
        grid=(t // tm,),
        in_specs=[rows(d), rows(d), rows(d), rows(SLOT, P_AG), rows(d), rows(d),
                  rows(d, P_MA), rows(d, P_MB), rows(d, P_MC), rows(d),
                  w_spec, w_spec, w_spec, w_spec],
        out_specs=rows(d),
        out_shape=jax.ShapeDtypeStruct((t, d), F32),
        scratch_shapes=[pltpu.VMEM((tm, d), BF16)],
        compiler_params=_cparams(("parallel",)),
        name="merge",
    )(oc, osel, ow, pa2d, yb, yc, pm, pm, pm, x2d, wpa, wpb, wpc, wo)


def _ffn_kernel(x_ref, g_ref, w1_ref, w3_ref, w2_ref, o_ref, h_s, acc_s, *, nf):
    f = pl.program_id(1)

    @pl.when(f == 0)
    def _():
        x = x_ref[...]
        inv = lax.rsqrt(jnp.mean(x * x, axis=-1, keepdims=True) + NORM_EPS)
        h_s[...] = ((x * inv) * g_ref[...]).astype(BF16)
        acc_s[...] = jnp.zeros(acc_s.shape, F32)

    h = h_s[...]
    u = _dot(h, w1_ref[...])
    z = (u * jax.nn.sigmoid(u)) * _dot(h, w3_ref[...])
    acc_s[...] += _dot(z.astype(BF16), w2_ref[...])

    @pl.when(f == nf - 1)
    def _():
        o_ref[...] = x_ref[...] + acc_s[...]


def ffn(x2d, g, w1, w3, w2, tm=1024, tf=256):
    t, d = x2d.shape
    hidden = w1.shape[1]
    nf = hidden // tf
    return pl.pallas_call(
        functools.partial(_ffn_kernel, nf=nf),
        grid=(t // tm, nf),
        in_specs=[pl.BlockSpec((tm, d), lambda i, f: (i, 0)),
                  pl.BlockSpec((1, d), lambda i, f: (0, 0)),
                  pl.BlockSpec((d, tf), lambda i, f: (0, f)),
                  pl.BlockSpec((d, tf), lambda i, f: (0, f)),
                  pl.BlockSpec((tf, d), lambda i, f: (f, 0))],
        out_specs=pl.BlockSpec((tm, d), lambda i, f: (i, 0)),
        out_shape=jax.ShapeDtypeStruct((t, d), F32),
        scratch_shapes=[pltpu.VMEM((tm, d), BF16), pltpu.VMEM((tm, d), F32)],
        compiler_params=_cparams(("parallel", "arbitrary")),
        name="ffn",
    )(x2d, g.reshape(1, d), w1, w3, w2)


def _pad_cols(w, width):
    return jnp.pad(w, ((0, 0), (0, width - w.shape[1])))


def _slot_cols(w, heads, real):
    k = w.shape[0]
    return jnp.pad(w.reshape(k, heads, real), ((0, 0), (0, 0), (0, SLOT - real))).reshape(k, heads * SLOT)


def _slot_rows(w, heads, real):
    n = w.shape[1]
    return jnp.pad(w.reshape(heads, real, n), ((0, 0), (0, SLOT - real), (0, 0))).reshape(heads * SLOT, n)


def _split_w_in(w):
    sizes = (A_Q, A_KV, A_KV, A_KV, A_KV, A_KV, A_KV, A_GATES, B_Q_LORA, B_KV_LORA, B_ROPE,
             C_WIDTH, C_WIDTH, D_MODEL, D_MODEL, D_MODEL)
    offs = np.cumsum((0,) + sizes)
    (aq, akc, avc, aks, avs, akw, avw, ag, bcq, bckv, bkpe, cg, cx, ma, mb, mc) = [
        w[:, offs[n]:offs[n + 1]] for n in range(len(sizes))]
    kpe_slot = jnp.pad(bkpe, ((0, 0), (B_NOPE, SLOT - B_NOPE - B_ROPE)))
    pieces = [(P_AQ, _slot_cols(aq, A_HEADS, A_HEAD_DIM)), (P_CG, cg), (P_CX, cx), (P_MA, ma), (P_MB, mb),
              (P_MC, mc), (P_KS, _slot_cols(aks, A_GROUPS, A_HEAD_DIM)), (P_VS, _slot_cols(avs, A_GROUPS, A_HEAD_DIM)),
              (P_KW, _slot_cols(akw, A_GROUPS, A_HEAD_DIM)), (P_VW, _slot_cols(avw, A_GROUPS, A_HEAD_DIM)),
              (P_CKV, bckv), (P_KC, akc), (P_VC, avc), (P_AG, _pad_cols(ag, SLOT)), (P_PE, kpe_slot), (P_CQ, bcq)]
    cols, at = [], 0
    for off, piece in pieces:
        assert off >= at
        if off > at:
            cols.append(jnp.zeros((w.shape[0], off - at), w.dtype))
        cols.append(piece)
        at = off + piece.shape[1]
    assert at == P_N
    return jnp.concatenate(cols, axis=1).astype(BF16)


def _rope_angles(positions, half):
    inv = ROPE_THETA ** (-jnp.arange(half, dtype=F32) / half)
    ang = positions.astype(F32)[..., None] * inv
    return jnp.cos(ang), jnp.sin(ang)


def _rope_tables(cos, sin, lo):
    shape, half = cos.shape[:-1], cos.shape[-1]
    tail = SLOT - lo - 2 * half
    cos_t = jnp.concatenate([jnp.ones(shape + (lo,), F32), cos, cos, jnp.ones(shape + (tail,), F32)], axis=-1)
    sin_t = jnp.concatenate([jnp.zeros(shape + (lo,), F32), sin, sin, jnp.zeros(shape + (tail,), F32)], axis=-1)
    return cos_t, sin_t


def _overlap_t(s):
    n_cmp = s // CMP_STRIDE
    n_sel = s // SEL_LEN
    starts = np.arange(n_cmp) * CMP_STRIDE
    sel_starts = np.arange(n_sel) * SEL_LEN
    ovl = np.clip(np.minimum(starts[:, None] + CMP_LEN, sel_starts[None, :] + SEL_LEN)
                  - np.maximum(starts[:, None], sel_starts[None, :]), 0, None) / CMP_LEN
    ovl[(s - CMP_LEN) // CMP_STRIDE + 1:, :] = 0.0
    return jnp.asarray(ovl.T, dtype=F32)


def _both_groups(w):
    z = jnp.zeros_like(w)
    return jnp.concatenate([jnp.concatenate([w, z], axis=-1), jnp.concatenate([z, w], axis=-1)], axis=-2)


def _vec_slot(g):
    return jnp.pad(g, (0, SLOT - g.shape[0])).reshape(1, SLOT).astype(F32)


def kernel(x, positions, mix_norm_g, w_in, a_q_norm_g, a_k_norm_g, a_cmp_pos, a_cmp_w1, a_cmp_w2, b_cq_norm_g, b_ckv_norm_g, b_w_uq, b_w_ukv, b_q_norm_g, b_k_norm_g, c_conv_w, c_conv_b, c_w_a, c_b_a, c_w_x, c_b_x, c_lambda, w_pa, w_pb, w_pc, w_o, ffn_norm_g, ffn_w1, ffn_w3, ffn_w2):
    bsz, s, d = x.shape
    t = bsz * s
    depth = w_in.shape[0]
    n_chunk = s // CMP_STRIDE
    tabs_a = _rope_tables(*_rope_angles(positions, A_HEAD_DIM // 2), 0)
    tabs_b = _rope_tables(*_rope_angles(positions, B_ROPE // 2), B_NOPE)
    ovl_t = _overlap_t(s)

    x2d = x.reshape(t, d)
    for l in range(depth):
        pj = norm_matmul(x2d, mix_norm_g[l], _split_w_in(w_in[l]))
        pj3 = pj.reshape(bsz, s, P_N)

        qg, kg = _vec_slot(a_q_norm_g[l]), _vec_slot(a_k_norm_g[l])
        qn, qr, ks, vs, kw, vw = nsa_prep(pj3, tabs_a, qg, kg)

        w1 = _both_groups(a_cmp_w1[l].reshape(2, CMP_LEN, A_HEAD_DIM, A_HEAD_DIM)).astype(BF16)
        w2 = _both_groups(a_cmp_w2[l]).astype(BF16)
        pos = jnp.concatenate([a_cmp_pos[l]] * A_GROUPS, axis=-1)
        pos = jnp.broadcast_to(pos[:, :, None, :], (2, CMP_LEN, SUBLANES, SLOT)).astype(BF16)
        kg2 = jnp.concatenate([a_k_norm_g[l]] * A_GROUPS).reshape(1, SLOT).astype(F32)
        kc, vc = nsa_compress(pj3.reshape(bsz, n_chunk, CMP_STRIDE, P_N), w1, w2, pos, kg2)
        oc, nsel = nsa_cmp_attn(qn, kc, vc, ovl_t)
        osel = flash_attention(qr, ks, vs, nsel, hb=A_GROUPS, hpg=A_HPG, real_v=A_HEAD_DIM, tq=512, tk=512)
        ow = window_attention(qr, kw, vw, hb=A_GROUPS, hpg=A_HPG, real_v=A_HEAD_DIM, tq=256)

        wuq = _slot_cols(b_w_uq[l], B_HEADS, B_QK).astype(BF16)
        ukv = b_w_ukv[l].reshape(B_KV_LORA, B_HEADS, B_NOPE + B_V)
        wuk = _slot_cols(ukv[:, :, :B_NOPE].reshape(B_KV_LORA, B_HEADS * B_NOPE), B_HEADS, B_NOPE).astype(BF16)
        wuv = _slot_cols(ukv[:, :, B_NOPE:].reshape(B_KV_LORA, B_HEADS * B_V), B_HEADS, B_V).astype(BF16)
        bq, bk, bv = mla_prep(pj3, b_cq_norm_g[l].reshape(1, -1), b_ckv_norm_g[l].reshape(1, -1),
                              wuq, wuk, wuv, tabs_b, _vec_slot(b_q_norm_g[l]), _vec_slot(b_k_norm_g[l]))
        yb = flash_attention(bq, bk, bv, hb=4, hpg=1, real_v=B_V, tq=1024, tk=1024)

        yc = rglru(pj3, c_conv_w[l], c_conv_b[l].reshape(1, -1),
                   c_w_a[l].astype(BF16), c_b_a[l].reshape(1, -1), c_w_x[l].astype(BF16),
                   c_b_x[l].reshape(1, -1), c_lambda[l].reshape(1, -1))

        x2d = merge(oc.reshape(t, -1), osel.reshape(t, -1), ow.reshape(t, -1), pj,
                    yb.reshape(t, -1), yc.reshape(t, -1), pj, x2d,
                    _slot_rows(w_pa[l], A_HEADS, A_HEAD_DIM).astype(BF16),
                    _slot_rows(w_pb[l], B_HEADS, B_V).astype(BF16),
                    w_pc[l].astype(BF16), w_o[l].astype(BF16))
        x2d = ffn(x2d, ffn_norm_g[l], ffn_w1[l].astype(BF16), ffn_w3[l].astype(BF16), ffn_w2[l].astype(BF16))
    return x2d.reshape(bsz, s, d)
```
